```python
import jax, jax.numpy as jnp
from jax import lax
import numpy as np

D_MODEL = 1024
BATCH = 2
SEQ = 8192
DEPTH = 2

GRID_W = 64
CTX_LEN = 256
N_MOD = 6
EPS = 1e-6
N_EVEN = (DEPTH + 1) // 2
N_ODD = DEPTH // 2
MLA_HEADS = 8
MLA_NOPE = 64
MLA_ROPE = 32
MLA_V = 64
MLA_Q_RANK = 256
MLA_KV_RANK = 128
MLA_SCALE = (MLA_NOPE + MLA_ROPE) ** -0.5
CONV_CH = 512
CONV_WIDTH = 31
AB_IN = MLA_Q_RANK + MLA_KV_RANK + MLA_ROPE + 2 * CONV_CH
AB_MIX = MLA_HEADS * MLA_V + CONV_CH
ROPE_BASE = 10000.0
Q_BLOCK = 128
HG_HEADS = 8
HG_DK = 128
HG_DV = D_MODEL // HG_HEADS
HG_F = HG_HEADS * HG_DK
HG_V = HG_HEADS * HG_DV
HG_IN = 3 * HG_F + 2 * HG_V
HG_CHUNK = 64
PEER_HEADS = 8
PEER_NKEYS = 128
PEER_EXPERTS = PEER_NKEYS * PEER_NKEYS
PEER_QDIM = 256
PEER_HALF = PEER_QDIM // 2
PEER_TOPK = 16
PEER_BLOCK = 128

kernel_name = 'hybrid_mla_conformer_hgrn2_peer_block'


def rms_norm(x, g):
    xf = x.astype(jnp.float32)
    y = xf * lax.rsqrt(jnp.mean(xf * xf, axis=-1, keepdims=True) + EPS)
    return (y * g.astype(jnp.float32)).astype(x.dtype)


def layer_norm(x, g, b):
    xf = x.astype(jnp.float32)
    mu = jnp.mean(xf, axis=-1, keepdims=True)
    xc = xf - mu
    y = xc * lax.rsqrt(jnp.mean(xc * xc, axis=-1, keepdims=True) + EPS)
    return (y * g.astype(jnp.float32) + b.astype(jnp.float32)).astype(x.dtype)


def modulate(h, shift, scale):
    return h * (1 + scale) + shift


def axial_rope_tables(n_tokens):
    rows = n_tokens // GRID_W
    row = jnp.repeat(jnp.arange(rows, dtype=jnp.float32), GRID_W)
    col = jnp.tile(jnp.arange(GRID_W, dtype=jnp.float32), rows)
    axis_dim = MLA_ROPE // 2
    inv = ROPE_BASE ** (-jnp.arange(0, axis_dim, 2, dtype=jnp.float32) / axis_dim)
    ang = jnp.concatenate([row[:, None] * inv, col[:, None] * inv], axis=-1)
    return jnp.cos(ang), jnp.sin(ang)


def apply_rope(x, cos, sin):
    xp = x.astype(jnp.float32).reshape(x.shape[:-1] + (x.shape[-1] // 2, 2))
    x0, x1 = xp[..., 0], xp[..., 1]
    out = jnp.stack([x0 * cos - x1 * sin, x0 * sin + x1 * cos], axis=-1)
    return out.reshape(x.shape).astype(x.dtype)


def mla_queries(cq, q_g, w_uq):
    B, T = cq.shape[:2]
    return (rms_norm(cq, q_g) @ w_uq).reshape(B, T, MLA_HEADS, MLA_NOPE + MLA_ROPE)


def mla_keys_values(ckv, k_rope, kv_g, w_ukv):
    B, T = ckv.shape[:2]
    kv = (rms_norm(ckv, kv_g) @ w_ukv).reshape(B, T, MLA_HEADS, MLA_NOPE + MLA_V)
    k = jnp.concatenate([kv[..., :MLA_NOPE], jnp.broadcast_to(k_rope[:, :, None, :], (B, T, MLA_HEADS, MLA_ROPE))], axis=-1)
    return k, kv[..., MLA_NOPE:]


def attend(q, k, v):
    s = jnp.einsum('bqhd,bkhd->bhqk', q, k).astype(jnp.float32) * MLA_SCALE
    p = jax.nn.softmax(s, axis=-1).astype(v.dtype)
    return jnp.einsum('bhqk,bkhd->bqhd', p, v)


def blocked_attention(q, k, v):
    B, T, H, dq = q.shape
    nb = T // Q_BLOCK
    qb = q.reshape(B, nb, Q_BLOCK, H, dq).transpose(1, 0, 2, 3, 4)
    ob = lax.map(lambda qi: attend(qi, k, v), qb)
    return ob.transpose(1, 0, 2, 3, 4).reshape(B, T, H, v.shape[-1])


def conformer_conv(u, conv_w, conv_b, ln_g, ln_b):
    a, gt = jnp.split(u, 2, axis=-1)
    y = a * jax.nn.sigmoid(gt)
    y = lax.conv_general_dilated(y, conv_w[:, None, :], window_strides=(1,),
                                 padding=[(CONV_WIDTH // 2, CONV_WIDTH // 2)],
                                 dimension_numbers=('NWC', 'WIO', 'NWC'),
                                 feature_group_count=CONV_CH) + conv_b
    return jax.nn.silu(layer_norm(y, ln_g, ln_b))


def mla_conv_mixer(h, hc, need_ctx_out, w_in, q_g, w_uq, kv_g, w_ukv, conv_w, conv_b, ln_g, ln_b, w_out):
    B, T = h.shape[:2]
    splits = [MLA_Q_RANK, MLA_Q_RANK + MLA_KV_RANK, MLA_Q_RANK + MLA_KV_RANK + MLA_ROPE]
    cq, ckv, kr, cu = jnp.split(h @ w_in, splits, axis=-1)
    cqc, ckvc, krc, cuc = jnp.split(hc @ w_in, splits, axis=-1)
    cos, sin = axial_rope_tables(T)
    q = mla_queries(cq, q_g, w_uq)
    q = jnp.concatenate([q[..., :MLA_NOPE], apply_rope(q[..., MLA_NOPE:], cos[:, None], sin[:, None])], axis=-1)
    k, v = mla_keys_values(ckv, apply_rope(kr, cos, sin), kv_g, w_ukv)
    kc, vc = mla_keys_values(ckvc, krc, kv_g, w_ukv)
    o = blocked_attention(q, jnp.concatenate([kc, k], axis=1), jnp.concatenate([vc, v], axis=1))
    y = jnp.concatenate([o.reshape(B, T, -1), conformer_conv(cu, conv_w, conv_b, ln_g, ln_b)], axis=-1) @ w_out
    if not need_ctx_out:
        return y, None
    oc = attend(mla_queries(cqc, q_g, w_uq), kc, vc)
    yc = jnp.concatenate([oc.reshape(B, hc.shape[1], -1), conformer_conv(cuc, conv_w, conv_b, ln_g, ln_b)], axis=-1) @ w_out
    return y, yc


def hgrn_lower_bounds(lb_param):
    p = jax.nn.softmax(lb_param.astype(jnp.float32), axis=0)
    cum = jnp.cumsum(p, axis=0)
    return cum - cum[0]


def gla_chunked(q, k, v, logf, s0):
    B, T, H, DK = q.shape
    DV = v.shape[-1]
    C = HG_CHUNK
    n = T // C
    q, k, v, logf = [a.reshape(B, n, C, H, a.shape[-1]) for a in (q, k, v, logf)]
    b = jnp.cumsum(logf, axis=2)
    b_last = b[:, :, -1:]
    q_in = q * jnp.exp(b)
    k_in = k * jnp.exp(-b)
    k_out = k * jnp.exp(b_last - b)
    mask = jnp.tril(jnp.ones((C, C), dtype=bool))
    a = jnp.where(mask, jnp.einsum('bnthk,bnshk->bnhts', q_in, k_in), 0.0)
    o_intra = jnp.einsum('bnhts,bnshv->bnthv', a, v)
    ds = jnp.einsum('bnshk,bnshv->bnhkv', k_out, v)
    decay = jnp.exp(b_last[:, :, 0])

    def step(s, inp):
        d, dsn = inp
        return d[..., None] * s + dsn, s

    s_final, s_prev = lax.scan(step, s0, (jnp.moveaxis(decay, 1, 0), jnp.moveaxis(ds, 1, 0)))
    s_prev = jnp.moveaxis(s_prev, 0, 1)
    o_inter = jnp.einsum('bnthk,bnhkv->bnthv', q_in, s_prev)
    return (o_intra + o_inter).reshape(B, T, H, DV), s_final


def hgrn_mixer(h, hc, layer_idx, need_ctx_out, w_in, lb_param, norm_g, w_out):
    lb = hgrn_lower_bounds(lb_param)[layer_idx]

    def heads(a, d):
        return a.reshape(a.shape[:2] + (HG_HEADS, d))

    def prep(hh):
        p = (hh @ w_in).astype(jnp.float32)
        q, zf, zb, i, g = jnp.split(p, [HG_F, 2 * HG_F, 3 * HG_F, 3 * HG_F + HG_V], axis=-1)
        ff = lb[0] + (1.0 - lb[0]) * jax.nn.sigmoid(zf)
        fb = lb[1] + (1.0 - lb[1]) * jax.nn.sigmoid(zb)
        fwd = (heads(jnp.log(ff), HG_DK), heads(1.0 - ff, HG_DK))
        bwd = (heads(jnp.log(fb), HG_DK), heads(1.0 - fb, HG_DK))
        return heads(q, HG_DK), heads(i, HG_DV), g, fwd, bwd

    def flip(a):
        return jnp.flip(a, axis=1)

    def readout(o, g, dtype):
        on = rms_norm(o, jnp.ones((HG_DV,), jnp.float32)).reshape(o.shape[:2] + (HG_V,))
        return ((on * norm_g * jax.nn.silu(g)).astype(dtype)) @ w_out

    B = h.shape[0]
    s0 = jnp.zeros((B, HG_HEADS, HG_DK, HG_DV), jnp.float32)
    qc, ic, gc, (lfc, kfc), (lbc, kbc) = prep(hc)
    oc_f, sc_f = gla_chunked(qc, kfc, ic, lfc, s0)
    oc_b, sc_b = gla_chunked(flip(qc), flip(kbc), flip(ic), flip(lbc), s0)
    q, i, g, (lf, kf), (lbk, kb) = prep(h)
    o_f, _ = gla_chunked(q, kf, i, lf, sc_f)
    o_b, _ = gla_chunked(flip(q), flip(kb), flip(i), flip(lbk), sc_b)
    y = readout(o_f + flip(o_b), g, h.dtype)
    if not need_ctx_out:
        return y, None
    return y, readout(oc_f + flip(oc_b), gc, hc.dtype)


def peer_ffn(h, w_q, sub_keys, u, v):
    shp = h.shape
    x = h.reshape(-1, shp[-1])
    n = x.shape[0]
    q = (x @ w_q).reshape(n, PEER_HEADS, 2, PEER_HALF)
    s = jnp.einsum('nhpd,hpkd->nhpk', q, sub_keys).astype(jnp.float32)
    sv, si = lax.top_k(s, PEER_TOPK)
    cand = (sv[:, :, 0, :, None] + sv[:, :, 1, None, :]).reshape(n, PEER_HEADS, PEER_TOPK * PEER_TOPK)
    cidx = (si[:, :, 0, :, None] * PEER_NKEYS + si[:, :, 1, None, :]).reshape(n, PEER_HEADS, PEER_TOPK * PEER_TOPK)
    top_s, pos = lax.top_k(cand, PEER_TOPK)
    eidx = jnp.take_along_axis(cidx, pos, axis=-1)
    gate = jax.nn.softmax(top_s, axis=-1).astype(x.dtype)
    nb = n // PEER_BLOCK

    def block(args):
        xb, eb, gb = args
        act = jax.nn.gelu(jnp.einsum('nd,nhkd->nhk', xb, u[eb]), approximate=False)
        return jnp.einsum('nhk,nhkd->nd', gb * act, v[eb])

    y = lax.map(block, (x.reshape(nb, PEER_BLOCK, shp[-1]),
                        eidx.reshape(nb, PEER_BLOCK, PEER_HEADS, PEER_TOPK),
                        gate.reshape(nb, PEER_BLOCK, PEER_HEADS, PEER_TOPK)))
    return y.reshape(shp)


def setup_inputs(seed: int = 0) -> dict:
    key = jax.random.key(seed)
    ks = iter(jax.random.split(key, 32))

    def nrm(shape, scale):
        return jax.random.normal(next(ks), shape, jnp.float32) * scale

    def gain(shape):
        return 1.0 + nrm(shape, 0.02)

    d = D_MODEL
    return {
        'x': nrm((BATCH, SEQ, d), 1.0),
        'c': nrm((BATCH, d), 1.0),
        'ctx': nrm((BATCH, CTX_LEN, d), 1.0),
        'c_ctx': nrm((d,), 1.0),
        'ada_w': nrm((DEPTH, d, N_MOD * d), 0.5 * d ** -0.5),
        'ada_b': nrm((DEPTH, N_MOD * d), 0.02),
        'norm1_g': gain((DEPTH, d)),
        'norm2_g': gain((DEPTH, d)),
        'ab_w_in': nrm((N_EVEN, d, AB_IN), d ** -0.5),
        'mla_q_norm_g': gain((N_EVEN, MLA_Q_RANK)),
        'mla_w_uq': nrm((N_EVEN, MLA_Q_RANK, MLA_HEADS * (MLA_NOPE + MLA_ROPE)), MLA_Q_RANK ** -0.5),
        'mla_kv_norm_g': gain((N_EVEN, MLA_KV_RANK)),
        'mla_w_ukv': nrm((N_EVEN, MLA_KV_RANK, MLA_HEADS * (MLA_NOPE + MLA_V)), MLA_KV_RANK ** -0.5),
        'conv_w': nrm((N_EVEN, CONV_WIDTH, CONV_CH), CONV_WIDTH ** -0.5),
        'conv_b': nrm((N_EVEN, CONV_CH), 0.02),
        'conv_ln_g': gain((N_EVEN, CONV_CH)),
        'conv_ln_b': nrm((N_EVEN, CONV_CH), 0.02),
        'ab_w_out': nrm((N_EVEN, AB_MIX, d), AB_MIX ** -0.5),
        'hg_w_in': nrm((N_ODD, d, HG_IN), d ** -0.5),
        'hg_lower_bound': nrm((DEPTH, 2, HG_F), 0.1),
        'hg_norm_g': gain((N_ODD, HG_V)),
        'hg_w_out': nrm((N_ODD, HG_V, d), HG_V ** -0.5),
        'peer_w_q': nrm((DEPTH, d, PEER_HEADS * PEER_QDIM), d ** -0.5),
        'peer_sub_keys': nrm((DEPTH, PEER_HEADS, 2, PEER_NKEYS, PEER_HALF), PEER_HALF ** -0.5),
        'peer_u': nrm((DEPTH, PEER_EXPERTS, d), d ** -0.5),
        'peer_v': nrm((DEPTH, PEER_EXPERTS, d), 0.25),
        'final_norm_g': gain((d,)),
    }


def reference(x, c, ctx, c_ctx, ada_w, ada_b, norm1_g, norm2_g, ab_w_in, mla_q_norm_g, mla_w_uq,
              mla_kv_norm_g, mla_w_ukv, conv_w, conv_b, conv_ln_g, conv_ln_b, ab_w_out, hg_w_in,
              hg_lower_bound, hg_norm_g, hg_w_out, peer_w_q, peer_sub_keys, peer_u, peer_v, final_norm_g):
    xc = ctx
    sc = jax.nn.silu(c)
    scc = jax.nn.silu(c_ctx)
    for l in range(DEPTH):
        need_ctx = l < DEPTH - 1
        mod = jnp.split((sc @ ada_w[l] + ada_b[l])[:, None, :], N_MOD, axis=-1)
        modc = jnp.split(scc @ ada_w[l] + ada_b[l], N_MOD, axis=-1)
        h = modulate(rms_norm(x, norm1_g[l]), mod[0], mod[1])
        hc = modulate(rms_norm(xc, norm1_g[l]), modc[0], modc[1])
        if l % 2 == 0:
            e = l // 2
            y, yc = mla_conv_mixer(h, hc, need_ctx, ab_w_in[e], mla_q_norm_g[e], mla_w_uq[e], mla_kv_norm_g[e],
                                   mla_w_ukv[e], conv_w[e], conv_b[e], conv_ln_g[e], conv_ln_b[e], ab_w_out[e])
        else:
            o = l // 2
            y, yc = hgrn_mixer(h, hc, l, need_ctx, hg_w_in[o], hg_lower_bound, hg_norm_g[o], hg_w_out[o])
        x = x + mod[2] * y
        h2 = modulate(rms_norm(x, norm2_g[l]), mod[3], mod[4])
        x = x + mod[5] * peer_ffn(h2, peer_w_q[l], peer_sub_keys[l], peer_u[l], peer_v[l])
        if need_ctx:
            xc = xc + modc[2] * yc
            hc2 = modulate(rms_norm(xc, norm2_g[l]), modc[3], modc[4])
            xc = xc + modc[5] * peer_ffn(hc2, peer_w_q[l], peer_sub_keys[l], peer_u[l], peer_v[l])
    return rms_norm(x, final_norm_g)
```

```python
import functools

import jax
import jax.numpy as jnp
from jax import lax
from jax.experimental import pallas as pl
from jax.experimental.pallas import tpu as pltpu

F32 = jnp.float32
BF16 = jnp.bfloat16

D = 1024
EPS = 1e-6
GRID_W = 64
ROPE_BASE = 10000.0
MLA_HEADS, MLA_NOPE, MLA_ROPE, MLA_V = 8, 64, 32, 64
MLA_Q_RANK, MLA_KV_RANK = 256, 128
MLA_SCALE = (MLA_NOPE + MLA_ROPE) ** -0.5
CONV_CH, CONV_WIDTH = 512, 31
HEAD_PAD = 128
HG_HEADS, HG_DK, HG_DV, HG_CHUNK = 8, 128, 128, 64
PEER_HEADS, PEER_NKEYS, PEER_HALF, PEER_TOPK = 8, 128, 128, 16
PEER_EXPERTS = PEER_NKEYS * PEER_NKEYS
INV_SQRT2 = 0.7071067811865476

ROW_TILE = 256
VMEM_LIMIT = 56 * 1024 * 1024

NT_DIMS = (((1,), (1,)), ((), ()))
TN_DIMS = (((0,), (0,)), ((), ()))


def _cparams(sem, vmem=None):
    return pltpu.CompilerParams(dimension_semantics=sem, vmem_limit_bytes=vmem)


def _dot(a, b):
    return jnp.dot(a, b, preferred_element_type=F32)


def _rms(x, g):
    return x * lax.rsqrt(jnp.mean(x * x, axis=-1, keepdims=True) + EPS) * g


def _silu(x):
    return x * jax.nn.sigmoid(x)


def _mod_row(mods_ref, b, is_ctx):
    r = jnp.where(is_ctx, 2, b)
    return mods_ref[pl.ds(r, 1), :]


def _mod(row, k):
    return row[:, k * D:(k + 1) * D]


def _mods_kernel(c_ref, w_ref, b_ref, o_ref):
    s = _silu(c_ref[...])
    o_ref[0] = jnp.dot(s, w_ref[0], preferred_element_type=F32,
                       precision=lax.Precision.HIGHEST) + b_ref[0]


def _mods(c8, ada_w, ada_b):
    depth = ada_w.shape[0]
    nb = 6 * D // 1024
    return pl.pallas_call(
        _mods_kernel,
        grid=(depth, nb),
        in_specs=[pl.BlockSpec((8, D), lambda l, j: (0, 0)),
                  pl.BlockSpec((1, D, 1024), lambda l, j: (l, 0, j)),
                  pl.BlockSpec((1, 1, 1024), lambda l, j: (l, 0, j))],
        out_specs=pl.BlockSpec((1, 8, 1024), lambda l, j: (l, 0, j)),
        out_shape=jax.ShapeDtypeStruct((depth, 8, 6 * D), F32),
        compiler_params=_cparams(("parallel", "parallel")),
        name="mods",
    )(c8, ada_w, ada_b.reshape(depth, 1, 6 * D))


def _l0_in_kernel(x_ref, mods_ref, g1_ref, w1_ref, w2_ref, qg_ref, wq_ref, wqs_ref, kvg_ref, wk_ref,
                  wv_ref, e_ref, ct_ref, st_ref, q_ref, k_ref, v_ref, y_ref, *, n_x_tiles):
    b, t = pl.program_id(0), pl.program_id(1)
    row = _mod_row(mods_ref, b, t >= n_x_tiles)
    h = _rms(x_ref[0], g1_ref[...]) * (1.0 + _mod(row, 1)) + _mod(row, 0)
    hb = h.astype(BF16)
    u1 = _dot(hb, w1_ref[...])
    u2 = _dot(hb, w2_ref[...])
    y_ref[0] = u2[:, :CONV_CH] * jax.nn.sigmoid(u2[:, CONV_CH:])
    ct, st = ct_ref[...], st_ref[...]
    cqn = _rms(u1[:, :MLA_Q_RANK], qg_ref[...]).astype(BF16)
    q = _dot(cqn, wq_ref[...])
    qs = _dot(cqn, wqs_ref[...])
    for hh in range(MLA_HEADS):
        sl = slice(hh * HEAD_PAD, (hh + 1) * HEAD_PAD)
        q_ref[0, :, sl] = ((q[:, sl] * ct + qs[:, sl] * st) * MLA_SCALE).astype(BF16)
    c0 = MLA_Q_RANK
    ckvn = _rms(u1[:, c0:c0 + MLA_KV_RANK], kvg_ref[...]).astype(BF16)
    c1 = c0 + MLA_KV_RANK
    kr = (u1[:, c1:c1 + HEAD_PAD] * ct + u1[:, c1 + HEAD_PAD:c1 + 2 * HEAD_PAD] * st).astype(BF16)
    k_ref[0] = (_dot(ckvn, wk_ref[...]) + _dot(kr, e_ref[...])).astype(BF16)
    v_ref[0] = _dot(ckvn, wv_ref[...]).astype(BF16)


def _l0_in(xx, mods0, g1, w, ct, st, n_x_tiles):
    B, Tt, _ = xx.shape
    tm = ROW_TILE
    full = lambda a: pl.BlockSpec(a.shape, lambda b, t: (0,) * a.ndim)
    hw = MLA_HEADS * HEAD_PAD
    row_out = lambda n: pl.BlockSpec((1, tm, n), lambda b, t: (b, t, 0))
    weights = (w["w1"], w["w2"], w["qg"], w["wq"], w["wqs"], w["kvg"], w["wk"], w["wv"], w["e"])
    return pl.pallas_call(
        functools.partial(_l0_in_kernel, n_x_tiles=n_x_tiles),
        grid=(B, Tt // tm),
        in_specs=[row_out(D), full(mods0), full(g1)] + [full(a) for a in weights] + [
            pl.BlockSpec((tm, HEAD_PAD), lambda b, t: (t, 0)),
            pl.BlockSpec((tm, HEAD_PAD), lambda b, t: (t, 0))],
        out_specs=[row_out(hw), row_out(hw), row_out(hw), row_out(CONV_CH)],
        out_shape=[jax.ShapeDtypeStruct((B, Tt, hw), BF16)] * 3 + [jax.ShapeDtypeStruct((B, Tt, CONV_CH), F32)],
        compiler_params=_cparams(("parallel", "parallel"), VMEM_LIMIT),
        name="l0_in",
    )(xx, mods0, g1, *weights, ct, st)


def _attend(q, k_ref, v_ref, start, n_chunks, kc):
    tq = q.shape[0]

    def body(i, carry):
        m, l, acc = carry
        off = pl.multiple_of(start + i * kc, kc)
        k = k_ref[0, pl.ds(off, kc), :]
        v = v_ref[0, pl.ds(off, kc), :]
        s = lax.dot_general(q, k, NT_DIMS, preferred_element_type=F32)
        m_new = jnp.maximum(m, jnp.max(s, axis=-1, keepdims=True))
        p = jnp.exp(s - m_new)
        alpha = jnp.exp(m - m_new)
        l = alpha * l + jnp.sum(p, axis=-1, keepdims=True)
        acc = alpha * acc + _dot(p.astype(BF16), v)
        return m_new, l, acc

    init = (jnp.full((tq, 1), -jnp.inf, F32), jnp.zeros((tq, 1), F32), jnp.zeros((tq, HEAD_PAD), F32))
    _, l, acc = lax.fori_loop(0, n_chunks, body, init)
    return (acc / l).astype(BF16)


def _attn_kernel(q_ref, k_ref, v_ref, o_ref, *, n_x_tiles, t_all, t_ctx, kc):
    t = pl.program_id(2)

    @pl.when(t < n_x_tiles)
    def _():
        o_ref[0] = _attend(q_ref[0], k_ref, v_ref, 0, t_all // kc, kc)

    @pl.when(t >= n_x_tiles)
    def _():
        o_ref[0] = _attend(q_ref[0], k_ref, v_ref, t_all - t_ctx, 1, t_ctx)


def _attention(q, k, v, n_x_tiles, t_ctx):
    B, Tt, hw = q.shape
    tq = ROW_TILE
    kc = 768 if Tt % 768 == 0 else ROW_TILE
    qspec = pl.BlockSpec((1, tq, HEAD_PAD), lambda b, h, t: (b, t, h))
    kspec = pl.BlockSpec((1, Tt, HEAD_PAD), lambda b, h, t: (b, 0, h))
    return pl.pallas_call(
        functools.partial(_attn_kernel, n_x_tiles=n_x_tiles, t_all=Tt, t_ctx=t_ctx, kc=kc),
        grid=(B, MLA_HEADS, Tt // tq),
        in_specs=[qspec, kspec, kspec],
        out_specs=qspec,
        out_shape=jax.ShapeDtypeStruct((B, Tt, hw), BF16),
        compiler_params=_cparams(("parallel", "parallel", "parallel"), VMEM_LIMIT),
        name="attn",
    )(q, k, v)


def _conv_kernel(prev_ref, cur_ref, next_ref, w_ref, b_ref, g_ref, beta_ref, o_ref, buf_ref, *, n_x_tiles):
    t, nt = pl.program_id(1), pl.num_programs(1)
    tm = cur_ref.shape[1]
    pad = 16
    pv = jnp.where((t == 0) | (t == n_x_tiles), 0.0, 1.0)
    nv = jnp.where((t == n_x_tiles - 1) | (t == nt - 1), 0.0, 1.0)
    buf_ref[0:pad, :] = prev_ref[0, tm - pad:tm, :] * pv
    buf_ref[pad:pad + tm, :] = cur_ref[0]
    buf_ref[pad + tm:2 * pad + tm, :] = next_ref[0, 0:pad, :] * nv
    acc = jnp.zeros((tm, CONV_CH), F32)
    half = CONV_WIDTH // 2
    for k in range(CONV_WIDTH):
        acc = acc + buf_ref[pad - half + k:pad - half + k + tm, :] * w_ref[k:k + 1, :]
    y = acc + b_ref[...]
    mu = jnp.mean(y, axis=-1, keepdims=True)
    yc = y - mu
    yn = yc * lax.rsqrt(jnp.mean(yc * yc, axis=-1, keepdims=True) + EPS) * g_ref[...] + beta_ref[...]
    o_ref[0] = _silu(yn).astype(BF16)


def _conv(y, conv_w, conv_b, ln_g, ln_b, n_x_tiles):
    B, Tt, _ = y.shape
    tm = ROW_TILE
    nt = Tt // tm
    full = lambda a: pl.BlockSpec(a.shape, lambda b, t: (0,) * a.ndim)
    return pl.pallas_call(
        functools.partial(_conv_kernel, n_x_tiles=n_x_tiles),
        grid=(B, nt),
        in_specs=[pl.BlockSpec((1, tm, CONV_CH), lambda b, t: (b, jnp.maximum(t - 1, 0), 0)),
                  pl.BlockSpec((1, tm, CONV_CH), lambda b, t: (b, t, 0)),
                  pl.BlockSpec((1, tm, CONV_CH), lambda b, t: (b, jnp.minimum(t + 1, nt - 1), 0)),
                  full(conv_w), full(conv_b), full(ln_g), full(ln_b)],
        out_specs=pl.BlockSpec((1, tm, CONV_CH), lambda b, t: (b, t, 0)),
        out_shape=jax.ShapeDtypeStruct((B, Tt, CONV_CH), BF16),
        scratch_shapes=[pltpu.VMEM((tm + 32, CONV_CH), F32)],
        compiler_params=_cparams(("parallel", "parallel")),
        name="conv",
    )(y, y, y, conv_w, conv_b, ln_g, ln_b)


def _peer_q_part(x, row, g2_ref, wqt_ref, h2_ref, qt_ref):
    h2 = (_rms(x, g2_ref[...]) * (1.0 + _mod(row, 4)) + _mod(row, 3)).astype(BF16)
    h2_ref[0] = h2
    qt_ref[0] = lax.dot_general(wqt_ref[...], h2, NT_DIMS, preferred_element_type=F32)


def _l0_out_kernel(o_ref, cv_ref, x_ref, mods_ref, woa_ref, woc_ref, g2_ref, wqt_ref,
                   x1_ref, h2_ref, qt_ref, *, n_x_tiles):
    b, t = pl.program_id(0), pl.program_id(1)
    row = _mod_row(mods_ref, b, t >= n_x_tiles)
    y = _dot(o_ref[0], woa_ref[...]) + _dot(cv_ref[0], woc_ref[...])
    x1 = x_ref[0] + _mod(row, 2) * y
    x1_ref[0] = x1
    _peer_q_part(x1, row, g2_ref, wqt_ref, h2_ref, qt_ref)


def _peer_q_outs(B, Tt):
    tm = ROW_TILE
    nq = PEER_HEADS * 2 * PEER_HALF
    specs = [pl.BlockSpec((1, tm, D), lambda b, t: (b, t, 0)),
             pl.BlockSpec((1, tm, D), lambda b, t: (b, t, 0)),
             pl.BlockSpec((1, nq, tm), lambda b, t: (b, 0, t))]
    shapes = [jax.ShapeDtypeStruct((B, Tt, D), F32), jax.ShapeDtypeStruct((B, Tt, D), BF16),
              jax.ShapeDtypeStruct((B, nq, Tt), F32)]
    return specs, shapes


def _l0_out(o, cv, xx, mods0, woa, woc, g2, wqt, n_x_tiles):
    B, Tt, _ = xx.shape
    tm = ROW_TILE
    full = lambda a: pl.BlockSpec(a.shape, lambda b, t: (0,) * a.ndim)
    row = lambda n: pl.BlockSpec((1, tm, n), lambda b, t: (b, t, 0))
    out_specs, out_shape = _peer_q_outs(B, Tt)
    return pl.pallas_call(
        functools.partial(_l0_out_kernel, n_x_tiles=n_x_tiles),
        grid=(B, Tt // tm),
        in_specs=[row(o.shape[-1]), row(CONV_CH), row(D), full(mods0), full(woa), full(woc), full(g2), full(wqt)],
        out_specs=out_specs, out_shape=out_shape,
        compiler_params=_cparams(("parallel", "parallel"), VMEM_LIMIT),
        name="l0_out",
    )(o, cv, xx, mods0, woa, woc, g2, wqt)


def _top16(work, sorted_ref):
    code = jnp.zeros_like(work)
    for p in range(PEER_TOPK):
        m = jnp.max(work, axis=0, keepdims=True)
        sel = work == m
        code = jnp.where(sel, float(PEER_TOPK - p), code)
        work = jnp.where(sel, -jnp.inf, work)
        sorted_ref[p:p + 1, :] = m
    return code


def _route_kernel(qt_ref, sk_ref, thr_ref, ea_ref, code_ref, ebz_ref, as_ref, bs_ref):
    K = PEER_TOPK
    for hh in range(PEER_HEADS):
        r0 = hh * 2 * PEER_HALF
        qa = qt_ref[0, r0:r0 + PEER_HALF, :].astype(BF16)
        qb = qt_ref[0, r0 + PEER_HALF:r0 + 2 * PEER_HALF, :].astype(BF16)
        a = _dot(sk_ref[hh, 0], qa)
        b = _dot(sk_ref[hh, 1], qb)
        ra = _top16(a, as_ref)
        rb = _top16(b, bs_ref)
        av, bv = as_ref[...], bs_ref[...]
        cs = [av[p:p + 1, :] + bv for p in range(K)]
        work = list(cs)
        tau = None
        for it in range(K):
            m = functools.reduce(jnp.maximum, work)
            tau = jnp.max(m, axis=0, keepdims=True)
            if it < K - 1:
                work = [jnp.where(w == tau, -jnp.inf, w) for w in work]
        ea_s = jnp.exp(av - av[0:1, :])
        eb_s = jnp.exp(bv - bv[0:1, :])
        z = jnp.zeros_like(tau)
        thr = jnp.full_like(a, float(K + 1))
        for p in range(K):
            selp = cs[p] >= tau
            cnt = jnp.sum(jnp.where(selp, 1.0, 0.0), axis=0, keepdims=True)
            z = z + ea_s[p:p + 1, :] * jnp.sum(jnp.where(selp, eb_s, 0.0), axis=0, keepdims=True)
            thr = jnp.where(ra == float(K - p), float(K + 1) - cnt, thr)
        thr_ref[0, hh] = thr
        ea_ref[0, hh] = jnp.exp(a - av[0:1, :])
        code_ref[0, hh] = rb
        ebz_ref[0, hh] = jnp.exp(b - bv[0:1, :]) / z


def _route(qt, sk):
    B, nq, Tt = qt.shape
    tm = ROW_TILE
    spec = pl.BlockSpec((1, PEER_HEADS, PEER_NKEYS, tm), lambda b, t: (b, 0, 0, t))
    shape = jax.ShapeDtypeStruct((B, PEER_HEADS, PEER_NKEYS, Tt), F32)
    return pl.pallas_call(
        _route_kernel,
        grid=(B, Tt // tm),
        in_specs=[pl.BlockSpec((1, nq, tm), lambda b, t: (b, 0, t)),
                  pl.BlockSpec(sk.shape, lambda b, t: (0, 0, 0, 0))],
        out_specs=[spec] * 4, out_shape=[shape] * 4,
        scratch_shapes=[pltpu.VMEM((PEER_TOPK, tm), F32), pltpu.VMEM((PEER_TOPK, tm), F32)],
        compiler_params=_cparams(("parallel", "parallel")),
        name="peer_route",
    )(qt, sk)


def _dense_kernel(h2_ref, u_ref, vt_ref, thr_ref, ea_ref, code_ref, ebz_ref, o_ref, acc_ref, s_ref, a_ref,
                  *, lane_blk):
    e, ne = pl.program_id(2), pl.num_programs(2)
    te, tm = s_ref.shape
    nib = te // PEER_NKEYS

    @pl.when(e == 0)
    def _():
        acc_ref[...] = jnp.zeros_like(acc_ref)

    s_ref[...] = lax.dot_general(u_ref[...], h2_ref[0], NT_DIMS, preferred_element_type=F32)

    def ib_body(ib, carry):
        i = e * nib + ib
        r0 = pl.multiple_of(ib * PEER_NKEYS, PEER_NKEYS)
        for l0 in range(0, tm, lane_blk):
            ls = slice(l0, l0 + lane_blk)
            w = jnp.zeros((PEER_NKEYS, lane_blk), F32)
            for hh in range(PEER_HEADS):
                thr = thr_ref[0, hh, pl.ds(i, 1), ls]
                ea = ea_ref[0, hh, pl.ds(i, 1), ls]
                w = w + jnp.where(code_ref[0, hh, :, ls] >= thr, ebz_ref[0, hh, :, ls] * ea, 0.0)
            s = s_ref[pl.ds(r0, PEER_NKEYS), ls]
            g = 0.5 * s * (1.0 + lax.erf(s * INV_SQRT2))
            a_ref[pl.ds(r0, PEER_NKEYS), ls] = (w * g).astype(BF16)
        return carry

    lax.fori_loop(0, nib, ib_body, 0)
    acc_ref[...] += _dot(vt_ref[...], a_ref[...])

    @pl.when(e == ne - 1)
    def _():
        o_ref[0] = acc_ref[...].T


def _dense_tile(Tt):
    for tm in (768, 512, 384, 256, 128):
        if Tt % tm == 0:
            return tm
    raise ValueError(Tt)


def _dense(h2, u, vt, thr, ea, code, ebz):
    B, Tt, _ = h2.shape
    tm = _dense_tile(Tt)
    te = 512
    rspec = pl.BlockSpec((1, PEER_HEADS, PEER_NKEYS, tm), lambda b, t, e: (b, 0, 0, t))
    return pl.pallas_call(
        functools.partial(_dense_kernel, lane_blk=min(256, tm)),
        grid=(B, Tt // tm, PEER_EXPERTS // te),
        in_specs=[pl.BlockSpec((1, tm, D), lambda b, t, e: (b, t, 0)),
                  pl.BlockSpec((te, D), lambda b, t, e: (e, 0)),
                  pl.BlockSpec((D, te), lambda b, t, e: (0, e)),
                  rspec, rspec, rspec, rspec],
        out_specs=pl.BlockSpec((1, tm, D), lambda b, t, e: (b, t, 0)),
        out_shape=jax.ShapeDtypeStruct((B, Tt, D), F32),
        scratch_shapes=[pltpu.VMEM((D, tm), F32), pltpu.VMEM((te, tm), F32), pltpu.VMEM((te, tm), BF16)],
        compiler_params=_cparams(("parallel", "parallel", "arbitrary"), VMEM_LIMIT),
        name="peer_dense",
    )(h2, u, vt, thr, ea, code, ebz)


def _l1_in_kernel(x1_ref, peer_ref, mods0_ref, mods1_ref, g1_ref, w_ref, x2_ref, p_ref, *, n_x_tiles):
    b, t = pl.program_id(0), pl.program_id(1)
    ctx = t >= n_x_tiles
    row0 = _mod_row(mods0_ref, b, ctx)
    row1 = _mod_row(mods1_ref, b, ctx)
    x2 = x1_ref[0] + _mod(row0, 5) * peer_ref[0]
    x2_ref[0] = x2
    h = _rms(x2, g1_ref[...]) * (1.0 + _mod(row1, 1)) + _mod(row1, 0)
    p_ref[0] = _dot(h.astype(BF16), w_ref[...])


def _l1_in(x1, peer, mods0, mods1, g1, w, n_x_tiles):
    B, Tt, _ = x1.shape
    tm = ROW_TILE
    n = w.shape[1]
    full = lambda a: pl.BlockSpec(a.shape, lambda b, t: (0,) * a.ndim)
    row = lambda m: pl.BlockSpec((1, tm, m), lambda b, t: (b, t, 0))
    return pl.pallas_call(
        functools.partial(_l1_in_kernel, n_x_tiles=n_x_tiles),
        grid=(B, Tt // tm),
        in_specs=[row(D), row(D), full(mods0), full(mods1), full(g1), full(w)],
        out_specs=[row(D), row(n)],
        out_shape=[jax.ShapeDtypeStruct((B, Tt, D), F32), jax.ShapeDtypeStruct((B, Tt, n), F32)],
        compiler_params=_cparams(("parallel", "parallel"), VMEM_LIMIT),
        name="l1_in",
    )(x1, peer, mods0, mods1, g1, w)


def _gla_kernel(q_ref, z_ref, v_ref, lbp_ref, o_ref, st_ref, *, layer):
    d, s = pl.program_id(0), pl.program_id(2)
    C = HG_CHUNK

    @pl.when(s == 0)
    def _():
        st_ref[...] = jnp.zeros_like(st_ref)

    fwd = d == 0
    r = lax.broadcasted_iota(jnp.int32, (C, C), 0)
    c = lax.broadcasted_iota(jnp.int32, (C, C), 1)
    tri = jnp.where(fwd, c - r, r - c) <= 0
    tri_b = jnp.where(tri, 1.0, 0.0).astype(BF16)
    lbp = lbp_ref[0]
    ex = jnp.exp(lbp - jnp.max(lbp, axis=0, keepdims=True))
    lbv = jnp.sum(ex[1:layer + 1], axis=0, keepdims=True) / jnp.sum(ex, axis=0, keepdims=True)
    for hh in range(HG_HEADS):
        sl = slice(hh * HG_DK, (hh + 1) * HG_DK)
        lb = lbv[:, sl]
        f = lb + (1.0 - lb) * jax.nn.sigmoid(z_ref[0, :, sl])
        logf = jnp.log(f)
        k = 1.0 - f
        hi = logf.astype(BF16)
        lo = (logf - hi.astype(F32)).astype(BF16)
        bc = _dot(tri_b, hi) + _dot(tri_b, lo)
        btot = jnp.where(fwd, bc[C - 1:C, :], bc[0:1, :])
        qi = (q_ref[0, :, sl] * jnp.exp(bc)).astype(BF16)
        ki = (k * jnp.exp(-bc)).astype(BF16)
        ko = (k * jnp.exp(btot - bc)).astype(BF16)
        vb = v_ref[0, :, sl].astype(BF16)
        a = lax.dot_general(qi, ki, NT_DIMS, preferred_element_type=F32)
        a = jnp.where(tri, a, 0.0).astype(BF16)
        st = st_ref[hh]
        o_ref[0, 0, :, sl] = _dot(a, vb) + lax.dot_general(qi, st.astype(BF16), NT_DIMS,
                                                            preferred_element_type=F32)
        st_ref[hh] = st * jnp.exp(btot) + lax.dot_general(vb, ko, TN_DIMS, preferred_element_type=F32)


def _gla(p, lbp, n_x_chunks, n_c_chunks, layer):
    B, Tt, _ = p.shape
    C = HG_CHUNK
    F = HG_HEADS * HG_DK
    nch = n_x_chunks + n_c_chunks

    def chunk(d, s):
        fwd = jnp.where(s < n_c_chunks, n_x_chunks + s, s - n_c_chunks)
        bwd = jnp.where(s < n_c_chunks, nch - 1 - s, nch - 1 - s)
        return jnp.where(d == 0, fwd, bwd)

    return pl.pallas_call(
        functools.partial(_gla_kernel, layer=layer),
        grid=(2, B, nch),
        in_specs=[pl.BlockSpec((1, C, F), lambda d, b, s: (b, chunk(d, s), 0)),
                  pl.BlockSpec((1, C, F), lambda d, b, s: (b, chunk(d, s), 1 + d)),
                  pl.BlockSpec((1, C, F), lambda d, b, s: (b, chunk(d, s), 3)),
                  pl.BlockSpec((1,) + lbp.shape[1:], lambda d, b, s: (d, 0, 0))],
        out_specs=pl.BlockSpec((1, 1, C, F), lambda d, b, s: (d, b, chunk(d, s), 0)),
        out_shape=jax.ShapeDtypeStruct((2, B, Tt, F), F32),
        scratch_shapes=[pltpu.VMEM((HG_HEADS, HG_DV, HG_DK), F32)],
        compiler_params=_cparams(("parallel", "parallel", "arbitrary")),
        name="gla",
    )(p, p, p, lbp)


def _l1_out_kernel(of_ref, ob_ref, g_ref, x_ref, mods_ref, ng_ref, wo_ref, g2_ref, wqt_ref,
                   x3_ref, h2_ref, qt_ref, *, n_x_tiles):
    b, t = pl.program_id(0), pl.program_id(1)
    row = _mod_row(mods_ref, b, t >= n_x_tiles)
    o = of_ref[0, 0] + ob_ref[0, 0]
    parts = []
    for hh in range(HG_HEADS):
        oh = o[:, hh * HG_DV:(hh + 1) * HG_DV]
        parts.append(oh * lax.rsqrt(jnp.mean(oh * oh, axis=-1, keepdims=True) + EPS))
    on = jnp.concatenate(parts, axis=1)
    y = (on * ng_ref[...] * _silu(g_ref[0])).astype(BF16)
    x3 = x_ref[0] + _mod(row, 2) * _dot(y, wo_ref[...])
    x3_ref[0] = x3
    _peer_q_part(x3, row, g2_ref, wqt_ref, h2_ref, qt_ref)


def _l1_out(o2, p, x2, mods1, ng, wo, g2, wqt, n_x_tiles):
    B, Tt, _ = x2.shape
    tm = ROW_TILE
    full = lambda a: pl.BlockSpec(a.shape, lambda b, t: (0,) * a.ndim)
    row = lambda n: pl.BlockSpec((1, tm, n), lambda b, t: (b, t, 0))
    out_specs, out_shape = _peer_q_outs(B, Tt)
    return pl.pallas_call(
        functools.partial(_l1_out_kernel, n_x_tiles=n_x_tiles),
        grid=(B, Tt // tm),
        in_specs=[pl.BlockSpec((1, 1, tm, D), lambda b, t: (0, b, t, 0)),
                  pl.BlockSpec((1, 1, tm, D), lambda b, t: (1, b, t, 0)),
                  pl.BlockSpec((1, tm, D), lambda b, t: (b, t, 4)),
                  row(D), full(mods1), full(ng), full(wo), full(g2), full(wqt)],
        out_specs=out_specs, out_shape=out_shape,
        compiler_params=_cparams(("parallel", "parallel"), VMEM_LIMIT),
        name="l1_out",
    )(o2, o2, p, x2, mods1, ng, wo, g2, wqt)


def _final_kernel(x_ref, peer_ref, mods_ref, g_ref, o_ref):
    b = pl.program_id(0)
    row = mods_ref[pl.ds(b, 1), :]
    o_ref[0] = _rms(x_ref[0] + _mod(row, 5) * peer_ref[0], g_ref[...])


def _final(x3, peer, mods1, g, T):
    B = x3.shape[0]
    tm = ROW_TILE
    full = lambda a: pl.BlockSpec(a.shape, lambda b, t: (0,) * a.ndim)
    row = pl.BlockSpec((1, tm, D), lambda b, t: (b, t, 0))
    return pl.pallas_call(
        _final_kernel,
        grid=(B, T // tm),
        in_specs=[row, row, full(mods1), full(g)],
        out_specs=row,
        out_shape=jax.ShapeDtypeStruct((B, T, D), F32),
        compiler_params=_cparams(("parallel", "parallel")),
        name="final",
    )(x3, peer, mods1, g)


def _rope_tables(T, Tc):
    rows = T // GRID_W
    row = jnp.repeat(jnp.arange(rows, dtype=F32), GRID_W)
    col = jnp.tile(jnp.arange(GRID_W, dtype=F32), rows)
    axis_dim = MLA_ROPE // 2
    inv = ROPE_BASE ** (-jnp.arange(0, axis_dim, 2, dtype=F32) / axis_dim)
    ang = jnp.concatenate([row[:, None] * inv, col[:, None] * inv], axis=-1)
    cos = jnp.concatenate([jnp.cos(ang), jnp.ones((Tc, axis_dim), F32)], axis=0)
    sin = jnp.concatenate([jnp.sin(ang), jnp.zeros((Tc, axis_dim), F32)], axis=0)
    Tt = T + Tc
    ct = jnp.concatenate([jnp.ones((Tt, MLA_NOPE), F32), cos, cos, jnp.ones((Tt, 32), F32)], axis=1)
    st = jnp.concatenate([jnp.zeros((Tt, MLA_NOPE), F32), -sin, sin, jnp.zeros((Tt, 32), F32)], axis=1)
    return ct, st


def _mla_weights(w_in, q_g, w_uq, kv_g, w_ukv):
    z = lambda *s: jnp.zeros(s, F32)
    c1 = MLA_Q_RANK + MLA_KV_RANK
    kr = w_in[:, c1:c1 + MLA_ROPE]
    kr_e, kr_o = kr[:, 0::2], kr[:, 1::2]
    krblk = jnp.concatenate([z(D, MLA_NOPE), kr_e, kr_o, z(D, 32)], axis=1)
    krsblk = jnp.concatenate([z(D, MLA_NOPE), kr_o, kr_e, z(D, 32)], axis=1)
    w1 = jnp.concatenate([w_in[:, :c1], krblk, krsblk], axis=1)
    w2 = w_in[:, c1 + MLA_ROPE:]
    wu = w_uq.reshape(MLA_Q_RANK, MLA_HEADS, MLA_NOPE + MLA_ROPE)
    nope, r = wu[:, :, :MLA_NOPE], wu[:, :, MLA_NOPE:]
    re, ro = r[:, :, 0::2], r[:, :, 1::2]
    zq = lambda n: z(MLA_Q_RANK, MLA_HEADS, n)
    wq = jnp.concatenate([nope, re, ro, zq(32)], axis=-1).reshape(MLA_Q_RANK, -1)
    wqs = jnp.concatenate([zq(MLA_NOPE), ro, re, zq(32)], axis=-1).reshape(MLA_Q_RANK, -1)
    wkv = w_ukv.reshape(MLA_KV_RANK, MLA_HEADS, MLA_NOPE + MLA_V)
    zk = z(MLA_KV_RANK, MLA_HEADS, HEAD_PAD - MLA_NOPE)
    wk = jnp.concatenate([wkv[:, :, :MLA_NOPE], zk], axis=-1).reshape(MLA_KV_RANK, -1)
    wv = jnp.concatenate([wkv[:, :, MLA_NOPE:], zk], axis=-1).reshape(MLA_KV_RANK, -1)
    lane = jnp.arange(HEAD_PAD)
    place = jnp.where((lane >= MLA_NOPE) & (lane < MLA_NOPE + MLA_ROPE), 1.0, 0.0)
    e = jnp.tile(jnp.diag(place), (1, MLA_HEADS))
    bf = lambda a: a.astype(BF16)
    return dict(w1=bf(w1), w2=bf(w2), qg=q_g.reshape(1, -1), wq=bf(wq), wqs=bf(wqs),
                kvg=kv_g.reshape(1, -1), wk=bf(wk), wv=bf(wv), e=bf(e))


def _out_weights(w_out):
    wo = w_out[:MLA_HEADS * MLA_V].reshape(MLA_HEADS, MLA_V, D)
    woa = jnp.concatenate([wo, jnp.zeros((MLA_HEADS, HEAD_PAD - MLA_V, D), F32)], axis=1).reshape(-1, D)
    return woa.astype(BF16), w_out[MLA_HEADS * MLA_V:].astype(BF16)


def _peer(h2, qt, sub_keys, u, v):
    thr, ea, code, ebz = _route(qt, sub_keys.astype(BF16))
    return _dense(h2, u.astype(BF16), v.T.astype(BF16), thr, ea, code, ebz)


def kernel(x, c, ctx, c_ctx, ada_w, ada_b, norm1_g, norm2_g, ab_w_in, mla_q_norm_g, mla_w_uq, mla_kv_norm_g, mla_w_ukv, conv_w, conv_b, conv_ln_g, conv_ln_b, ab_w_out, hg_w_in, hg_lower_bound, hg_norm_g, hg_w_out, peer_w_q, peer_sub_keys, peer_u, peer_v, final_norm_g):
    B, T, _ = x.shape
    Tc = ctx.shape[1]
    assert B <= 2 and T % ROW_TILE == 0 and Tc % ROW_TILE == 0 and ada_w.shape[0] == 2
    n_x_tiles = T // ROW_TILE
    row2 = lambda a: a.reshape(1, -1)

    xx = jnp.concatenate([x, ctx], axis=1)
    c8 = jnp.zeros((8, D), F32).at[:B].set(c).at[2].set(c_ctx)
    mods = _mods(c8, ada_w, ada_b)

    ct, st = _rope_tables(T, Tc)
    mw = _mla_weights(ab_w_in[0], mla_q_norm_g[0], mla_w_uq[0], mla_kv_norm_g[0], mla_w_ukv[0])
    q, k, v, y = _l0_in(xx, mods[0], row2(norm1_g[0]), mw, ct, st, n_x_tiles)
    o = _attention(q, k, v, n_x_tiles, Tc)
    cv = _conv(y, conv_w[0], row2(conv_b[0]), row2(conv_ln_g[0]), row2(conv_ln_b[0]), n_x_tiles)
    woa, woc = _out_weights(ab_w_out[0])
    x1, h2, qt = _l0_out(o, cv, xx, mods[0], woa, woc, row2(norm2_g[0]), peer_w_q[0].T.astype(BF16), n_x_tiles)
    peer0 = _peer(h2, qt, peer_sub_keys[0], peer_u[0], peer_v[0])

    x2, p = _l1_in(x1, peer0, mods[0], mods[1], row2(norm1_g[1]), hg_w_in[0].astype(BF16), n_x_tiles)
    o2 = _gla(p, jnp.transpose(hg_lower_bound, (1, 0, 2)), T // HG_CHUNK, Tc // HG_CHUNK, layer=1)
    x3, h2, qt = _l1_out(o2, p, x2, mods[1], row2(hg_norm_g[0]), hg_w_out[0].astype(BF16),
                         row2(norm2_g[1]), peer_w_q[1].T.astype(BF16), n_x_tiles)
    peer1 = _peer(h2, qt, peer_sub_keys[1], peer_u[1], peer_v[1])
    return _final(x3, peer1, mods[1], row2(final_norm_g), T)
```

```python
import functools

import jax
import jax.numpy as jnp
from jax import lax
from jax.experimental import pallas as pl
from jax.experimental.pallas import tpu as pltpu

F32 = jnp.float32
BF16 = jnp.bfloat16

D = 1024
EPS = 1e-6
GRID_W = 64
ROPE_BASE = 10000.0
MLA_HEADS, MLA_NOPE, MLA_ROPE, MLA_V = 8, 64, 32, 64
MLA_Q_RANK, MLA_KV_RANK = 256, 128
MLA_SCALE = (MLA_NOPE + MLA_ROPE) ** -0.5
LOG2E = 1.4426950408889634
CONV_CH, CONV_WIDTH = 512, 31
HEAD_PAD = 128
HG_HEADS, HG_DK, HG_DV, HG_CHUNK = 8, 128, 128, 64
PEER_HEADS, PEER_NKEYS, PEER_HALF, PEER_TOPK = 8, 128, 128, 16
PEER_EXPERTS = PEER_NKEYS * PEER_NKEYS
INV_SQRT2 = 0.7071067811865476

ROW_TILE = 256
VMEM_LIMIT = 56 * 1024 * 1024

NT_DIMS = (((1,), (1,)), ((), ()))
TN_DIMS = (((0,), (0,)), ((), ()))


def _cparams(sem, vmem=None):
    return pltpu.CompilerParams(dimension_semantics=sem, vmem_limit_bytes=vmem)


def _dot(a, b):
    return jnp.dot(a, b, preferred_element_type=F32)


def _rms(x, g):
    return x * lax.rsqrt(jnp.mean(x * x, axis=-1, keepdims=True) + EPS) * g


def _silu(x):
    return x * jax.nn.sigmoid(x)


def _mod_row(mods_ref, b, is_ctx):
    r = jnp.where(is_ctx, 2, b)
    return mods_ref[pl.ds(r, 1), :]


def _mod(row, k):
    return row[:, k * D:(k + 1) * D]


def _mods_kernel(c_ref, w_ref, b_ref, o_ref):
    s = _silu(c_ref[...])
    o_ref[0] = jnp.dot(s, w_ref[0], preferred_element_type=F32,
                       precision=lax.Precision.HIGHEST) + b_ref[0]


def _mods(c8, ada_w, ada_b):
    depth = ada_w.shape[0]
    nb = 6 * D // 1024
    return pl.pallas_call(
        _mods_kernel,
        grid=(depth, nb),
        in_specs=[pl.BlockSpec((8, D), lambda l, j: (0, 0)),
                  pl.BlockSpec((1, D, 1024), lambda l, j: (l, 0, j)),
                  pl.BlockSpec((1, 1, 1024), lambda l, j: (l, 0, j))],
        out_specs=pl.BlockSpec((1, 8, 1024), lambda l, j: (l, 0, j)),
        out_shape=jax.ShapeDtypeStruct((depth, 8, 6 * D), F32),
        compiler_params=_cparams(("parallel", "parallel")),
        name="mods",
    )(c8, ada_w, ada_b.reshape(depth, 1, 6 * D))


def _l0_in_kernel(x_ref, mods_ref, g1_ref, w1_ref, w2_ref, qg_ref, wq_ref, wqs_ref, kvg_ref, wk_ref,
                  wv_ref, e_ref, ct_ref, st_ref, q_ref, k_ref, v_ref, y_ref, *, n_x_tiles):
    b, t = pl.program_id(0), pl.program_id(1)
    row = _mod_row(mods_ref, b, t >= n_x_tiles)
    h = _rms(x_ref[0], g1_ref[...]) * (1.0 + _mod(row, 1)) + _mod(row, 0)
    hb = h.astype(BF16)
    u1 = _dot(hb, w1_ref[...])
    u2 = _dot(hb, w2_ref[...])
    y_ref[0] = u2[:, :CONV_CH] * jax.nn.sigmoid(u2[:, CONV_CH:])
    ct, st = ct_ref[...], st_ref[...]
    cqn = _rms(u1[:, :MLA_Q_RANK], qg_ref[...]).astype(BF16)
    q = _dot(cqn, wq_ref[...])
    qs = _dot(cqn, wqs_ref[...])
    qr = [(q[:, hh * HEAD_PAD:(hh + 1) * HEAD_PAD] * ct + qs[:, hh * HEAD_PAD:(hh + 1) * HEAD_PAD] * st)
          * (MLA_SCALE * LOG2E) for hh in range(MLA_HEADS)]
    q_ref[0] = jnp.concatenate(qr, axis=1).T.astype(BF16)
    c0 = MLA_Q_RANK
    ckvn = _rms(u1[:, c0:c0 + MLA_KV_RANK], kvg_ref[...]).astype(BF16)
    c1 = c0 + MLA_KV_RANK
    kr = (u1[:, c1:c1 + HEAD_PAD] * ct + u1[:, c1 + HEAD_PAD:c1 + 2 * HEAD_PAD] * st).astype(BF16)
    k_ref[0] = (_dot(ckvn, wk_ref[...]) + _dot(kr, e_ref[...])).astype(BF16)
    v_ref[0] = _dot(ckvn, wv_ref[...]).T.astype(BF16)


def _l0_in(xx, mods0, g1, w, ct, st, n_x_tiles):
    B, Tt, _ = xx.shape
    tm = ROW_TILE
    full = lambda a: pl.BlockSpec(a.shape, lambda b, t: (0,) * a.ndim)
    hw = MLA_HEADS * HEAD_PAD
    row_out = lambda n: pl.BlockSpec((1, tm, n), lambda b, t: (b, t, 0))
    col_out = lambda n: pl.BlockSpec((1, n, tm), lambda b, t: (b, 0, t))
    vw = MLA_HEADS * MLA_V
    weights = (w["w1"], w["w2"], w["qg"], w["wq"], w["wqs"], w["kvg"], w["wk"], w["wv"], w["e"])
    return pl.pallas_call(
        functools.partial(_l0_in_kernel, n_x_tiles=n_x_tiles),
        grid=(B, Tt // tm),
        in_specs=[row_out(D), full(mods0), full(g1)] + [full(a) for a in weights] + [
            pl.BlockSpec((tm, HEAD_PAD), lambda b, t: (t, 0)),
            pl.BlockSpec((tm, HEAD_PAD), lambda b, t: (t, 0))],
        out_specs=[col_out(hw), row_out(hw), col_out(vw), row_out(CONV_CH)],
        out_shape=[jax.ShapeDtypeStruct((B, hw, Tt), BF16), jax.ShapeDtypeStruct((B, Tt, hw), BF16),
                   jax.ShapeDtypeStruct((B, vw, Tt), BF16), jax.ShapeDtypeStruct((B, Tt, CONV_CH), F32)],
        compiler_params=_cparams(("parallel", "parallel"), VMEM_LIMIT),
        name="l0_in",
    )(xx, mods0, g1, *weights, ct, st)


def _attend(qt, k_ref, vt_ref, chunks):
    tq = qt.shape[1]
    m = jnp.full((1, tq), -jnp.inf, F32)
    l = jnp.zeros((1, tq), F32)
    acc = jnp.zeros((MLA_V, tq), F32)
    for off, kc in chunks:
        s = _dot(k_ref[0, off:off + kc, :], qt)
        m_new = jnp.maximum(m, jnp.max(s, axis=0, keepdims=True))
        p = jnp.exp2(s - m_new)
        alpha = jnp.exp2(m - m_new)
        l = alpha * l + jnp.sum(p, axis=0, keepdims=True)
        acc = alpha * acc + _dot(vt_ref[0, :, off:off + kc], p.astype(BF16))
        m = m_new
    return acc / l


def _attn_kernel(qt_ref, k_ref, vt_ref, o_ref, *, kc):
    tk = k_ref.shape[1]
    o_ref[0] = _attend(qt_ref[0], k_ref, vt_ref, [(o, kc) for o in range(0, tk, kc)]).astype(BF16)


def _attn_call(qt, k, vt, tq, n_q, q0, tk, k0, kc, name):
    B = k.shape[0]
    return pl.pallas_call(
        functools.partial(_attn_kernel, kc=kc),
        grid=(B, MLA_HEADS, n_q),
        in_specs=[pl.BlockSpec((1, HEAD_PAD, tq), lambda b, h, t: (b, h, q0 + t)),
                  pl.BlockSpec((1, tk, HEAD_PAD), lambda b, h, t: (b, k0, h)),
                  pl.BlockSpec((1, MLA_V, tk), lambda b, h, t: (b, h, k0))],
        out_specs=pl.BlockSpec((1, MLA_V, tq), lambda b, h, t: (b, h, t)),
        out_shape=jax.ShapeDtypeStruct((B, MLA_HEADS * MLA_V, n_q * tq), BF16),
        compiler_params=_cparams(("parallel", "parallel", "parallel"), VMEM_LIMIT),
        name=name,
    )(qt, k, vt)


def _attention(qt, k, vt, T, t_ctx):
    Tt = k.shape[1]
    tq = 512 if T % 512 == 0 else ROW_TILE
    kc = 768 if Tt % 768 == 0 else ROW_TILE
    o_x = _attn_call(qt, k, vt, tq, T // tq, 0, Tt, 0, kc, "attn")
    o_c = _attn_call(qt, k, vt, t_ctx, 1, T // t_ctx, t_ctx, T // t_ctx, t_ctx, "attn_ctx")
    return jnp.concatenate([o_x, o_c], axis=2)


def _conv_kernel(prev_ref, cur_ref, next_ref, w_ref, b_ref, g_ref, beta_ref, o_ref, buf_ref, *, n_x_tiles):
    t, nt = pl.program_id(1), pl.num_programs(1)
    tm = cur_ref.shape[1]
    pad = 16
    pv = jnp.where((t == 0) | (t == n_x_tiles), 0.0, 1.0)
    nv = jnp.where((t == n_x_tiles - 1) | (t == nt - 1), 0.0, 1.0)
    buf_ref[0:pad, :] = prev_ref[0, tm - pad:tm, :] * pv
    buf_ref[pad:pad + tm, :] = cur_ref[0]
    buf_ref[pad + tm:2 * pad + tm, :] = next_ref[0, 0:pad, :] * nv
    acc = jnp.zeros((tm, CONV_CH), F32)
    half = CONV_WIDTH // 2
    for k in range(CONV_WIDTH):
        acc = acc + buf_ref[pad - half + k:pad - half + k + tm, :] * w_ref[k:k + 1, :]
    y = acc + b_ref[...]
    mu = jnp.mean(y, axis=-1, keepdims=True)
    yc = y - mu
    yn = yc * lax.rsqrt(jnp.mean(yc * yc, axis=-1, keepdims=True) + EPS) * g_ref[...] + beta_ref[...]
    o_ref[0] = _silu(yn).astype(BF16)


def _conv(y, conv_w, conv_b, ln_g, ln_b, n_x_tiles):
    B, Tt, _ = y.shape
    tm = ROW_TILE
    nt = Tt // tm
    full = lambda a: pl.BlockSpec(a.shape, lambda b, t: (0,) * a.ndim)
    return pl.pallas_call(
        functools.partial(_conv_kernel, n_x_tiles=n_x_tiles),
        grid=(B, nt),
        in_specs=[pl.BlockSpec((1, tm, CONV_CH), lambda b, t: (b, jnp.maximum(t - 1, 0), 0)),
                  pl.BlockSpec((1, tm, CONV_CH), lambda b, t: (b, t, 0)),
                  pl.BlockSpec((1, tm, CONV_CH), lambda b, t: (b, jnp.minimum(t + 1, nt - 1), 0)),
                  full(conv_w), full(conv_b), full(ln_g), full(ln_b)],
        out_specs=pl.BlockSpec((1, tm, CONV_CH), lambda b, t: (b, t, 0)),
        out_shape=jax.ShapeDtypeStruct((B, Tt, CONV_CH), BF16),
        scratch_shapes=[pltpu.VMEM((tm + 32, CONV_CH), F32)],
        compiler_params=_cparams(("parallel", "parallel")),
        name="conv",
    )(y, y, y, conv_w, conv_b, ln_g, ln_b)


def _peer_q_part(x, row, g2_ref, wqt_ref, h2_ref, qt_ref):
    h2t = (_rms(x, g2_ref[...]) * (1.0 + _mod(row, 4)) + _mod(row, 3)).T.astype(BF16)
    h2_ref[...] = h2t
    qt_ref[...] = _dot(wqt_ref[...], h2t)


def _l0_out_kernel(o_ref, cv_ref, x_ref, mods_ref, woa_ref, woc_ref, g2_ref, wqt_ref,
                   x1_ref, h2_ref, qt_ref, *, n_x_tiles):
    b, t = pl.program_id(0), pl.program_id(1)
    row = _mod_row(mods_ref, b, t >= n_x_tiles)
    y = _dot(o_ref[0].T, woa_ref[...]) + _dot(cv_ref[0], woc_ref[...])
    x1 = x_ref[0] + _mod(row, 2) * y
    x1_ref[0] = x1
    _peer_q_part(x1, row, g2_ref, wqt_ref, h2_ref, qt_ref)


def _peer_q_outs(B, Tt):
    tm = ROW_TILE
    nq = PEER_HEADS * 2 * PEER_HALF
    nt = Tt // tm
    specs = [pl.BlockSpec((1, tm, D), lambda b, t: (b, t, 0)),
             pl.BlockSpec((D, tm), lambda b, t: (0, b * nt + t)),
             pl.BlockSpec((nq, tm), lambda b, t: (0, b * nt + t))]
    shapes = [jax.ShapeDtypeStruct((B, Tt, D), F32), jax.ShapeDtypeStruct((D, B * Tt), BF16),
              jax.ShapeDtypeStruct((nq, B * Tt), F32)]
    return specs, shapes


def _l0_out(o, cv, xx, mods0, woa, woc, g2, wqt, n_x_tiles):
    B, Tt, _ = xx.shape
    tm = ROW_TILE
    full = lambda a: pl.BlockSpec(a.shape, lambda b, t: (0,) * a.ndim)
    row = lambda n: pl.BlockSpec((1, tm, n), lambda b, t: (b, t, 0))
    out_specs, out_shape = _peer_q_outs(B, Tt)
    return pl.pallas_call(
        functools.partial(_l0_out_kernel, n_x_tiles=n_x_tiles),
        grid=(B, Tt // tm),
        in_specs=[pl.BlockSpec((1, o.shape[1], tm), lambda b, t: (b, 0, t)), row(CONV_CH), row(D), full(mods0),
                  full(woa), full(woc), full(g2), full(wqt)],
        out_specs=out_specs, out_shape=out_shape,
        compiler_params=_cparams(("parallel", "parallel"), VMEM_LIMIT),
        name="l0_out",
    )(o, cv, xx, mods0, woa, woc, g2, wqt)


def _top16(work, sorted_ref):
    code = jnp.zeros_like(work)
    for p in range(PEER_TOPK):
        m = jnp.max(work, axis=0, keepdims=True)
        sel = work == m
        code = jnp.where(sel, float(PEER_TOPK - p), code)
        work = jnp.where(sel, -jnp.inf, work)
        sorted_ref[p:p + 1, :] = m
    return code


def _pair_word(x):
    hi = lax.shift_right_logical(pltpu.bitcast(x.astype(BF16).astype(F32), jnp.uint32), jnp.uint32(16))
    return lax.shift_left(hi, jnp.uint32(16)) | hi


def _route_kernel(qt_ref, sk_ref, thr_ref, ea_ref, code_ref, ebz_ref, as_ref, bs_ref):
    K = PEER_TOPK
    for hh in range(PEER_HEADS):
        r0 = hh * 2 * PEER_HALF
        qa = qt_ref[r0:r0 + PEER_HALF, :].astype(BF16)
        qb = qt_ref[r0 + PEER_HALF:r0 + 2 * PEER_HALF, :].astype(BF16)
        a = _dot(sk_ref[hh, 0], qa)
        b = _dot(sk_ref[hh, 1], qb)
        ra = _top16(a, as_ref)
        rb = _top16(b, bs_ref)
        av, bv = as_ref[...], bs_ref[...]
        cs = [av[p:p + 1, :] + bv for p in range(K)]
        work = list(cs)
        tau = None
        for it in range(K):
            m = functools.reduce(jnp.maximum, work)
            tau = jnp.max(m, axis=0, keepdims=True)
            if it < K - 1:
                work = [jnp.where(w == tau, -jnp.inf, w) for w in work]
        ea_s = jnp.exp(av - av[0:1, :])
        eb_s = jnp.exp(bv - bv[0:1, :])
        z = jnp.zeros_like(tau)
        thr = jnp.full_like(a, float(K + 1))
        for p in range(K):
            selp = cs[p] >= tau
            cnt = jnp.sum(jnp.where(selp, 1.0, 0.0), axis=0, keepdims=True)
            z = z + ea_s[p:p + 1, :] * jnp.sum(jnp.where(selp, eb_s, 0.0), axis=0, keepdims=True)
            thr = jnp.where(ra == float(K - p), float(K + 1) - cnt, thr)
        thr_ref[hh] = _pair_word(thr)
        ea_ref[hh] = _pair_word(jnp.exp(a - av[0:1, :]))
        code_ref[hh] = rb.astype(BF16)
        ebz_ref[hh] = (jnp.exp(b - bv[0:1, :]) / z).astype(BF16)


def _route(qt, sk):
    nq, n = qt.shape
    tm = ROW_TILE
    spec = pl.BlockSpec((PEER_HEADS, PEER_NKEYS, tm), lambda t: (0, 0, t))
    shape = lambda dt: jax.ShapeDtypeStruct((PEER_HEADS, PEER_NKEYS, n), dt)
    return pl.pallas_call(
        _route_kernel,
        grid=(n // tm,),
        in_specs=[pl.BlockSpec((nq, tm), lambda t: (0, t)),
                  pl.BlockSpec(sk.shape, lambda t: (0, 0, 0, 0))],
        out_specs=[spec] * 4, out_shape=[shape(jnp.uint32), shape(jnp.uint32), shape(BF16), shape(BF16)],
        scratch_shapes=[pltpu.VMEM((PEER_TOPK, tm), F32), pltpu.VMEM((PEER_TOPK, tm), F32)],
        compiler_params=_cparams(("parallel",)),
        name="peer_route",
    )(qt, sk)


DENSE_SUB = 512
DENSE_TE = 1024


def _dense_kernel(h2t_ref, u_ref, vt_ref, thr_ref, ea_ref, code_ref, ebz_ref, o_ref, acc_ref):
    e, ne = pl.program_id(1), pl.num_programs(1)
    te = u_ref.shape[0]
    tm = h2t_ref.shape[1]
    nib = te // PEER_NKEYS
    pairs = PEER_NKEYS // 2

    @pl.when(e == 0)
    def _():
        acc_ref[...] = jnp.zeros_like(acc_ref)

    h2t = h2t_ref[...]
    zero = jnp.zeros((), BF16)
    for c in range(te // DENSE_SUB):
        rows = slice(c * DENSE_SUB, (c + 1) * DENSE_SUB)
        s = _dot(u_ref[rows, :], h2t)
        g = (s * (1.0 + lax.erf(s * INV_SQRT2))).astype(BF16)
        parts = []
        for ib in range(DENSE_SUB // PEER_NKEYS):
            i = e * nib + c * (DENSE_SUB // PEER_NKEYS) + ib
            w = jnp.zeros((PEER_NKEYS, tm), BF16)
            for hh in range(PEER_HEADS):
                thr = pltpu.bitcast(jnp.broadcast_to(thr_ref[hh, pl.ds(i, 1), :], (pairs, tm)), BF16)
                ea = pltpu.bitcast(jnp.broadcast_to(ea_ref[hh, pl.ds(i, 1), :], (pairs, tm)), BF16)
                w = w + jnp.where(code_ref[hh] >= thr, ebz_ref[hh] * ea, zero)
            parts.append(w * g[ib * PEER_NKEYS:(ib + 1) * PEER_NKEYS])
        acc_ref[...] += _dot(vt_ref[:, rows], jnp.concatenate(parts, axis=0))

    @pl.when(e == ne - 1)
    def _():
        o_ref[...] = acc_ref[...].T


def _dense_tile(n):
    for tm in (512, 256, 128):
        if n % tm == 0:
            return tm
    raise ValueError(n)


def _dense(h2t, u, vt_half, thr, ea, code, ebz):
    n = h2t.shape[1]
    tm = _dense_tile(n)
    te = DENSE_TE
    rspec = pl.BlockSpec((PEER_HEADS, PEER_NKEYS, tm), lambda t, e: (0, 0, t))
    return pl.pallas_call(
        _dense_kernel,
        grid=(n // tm, PEER_EXPERTS // te),
        in_specs=[pl.BlockSpec((D, tm), lambda t, e: (0, t)),
                  pl.BlockSpec((te, D), lambda t, e: (e, 0)),
                  pl.BlockSpec((D, te), lambda t, e: (0, e)),
                  rspec, rspec, rspec, rspec],
        out_specs=pl.BlockSpec((tm, D), lambda t, e: (t, 0)),
        out_shape=jax.ShapeDtypeStruct((n, D), F32),
        scratch_shapes=[pltpu.VMEM((D, tm), F32)],
        compiler_params=_cparams(("parallel", "arbitrary"), VMEM_LIMIT),
        name="peer_dense",
    )(h2t, u, vt_half, thr, ea, code, ebz)


def _l1_in_kernel(x1_ref, peer_ref, mods0_ref, mods1_ref, g1_ref, w_ref, x2_ref, p_ref, *, n_x_tiles):
    b, t = pl.program_id(0), pl.program_id(1)
    ctx = t >= n_x_tiles
    row0 = _mod_row(mods0_ref, b, ctx)
    row1 = _mod_row(mods1_ref, b, ctx)
    x2 = x1_ref[0] + _mod(row0, 5) * peer_ref[0]
    x2_ref[0] = x2
    h = _rms(x2, g1_ref[...]) * (1.0 + _mod(row1, 1)) + _mod(row1, 0)
    p_ref[0] = _dot(h.astype(BF16), w_ref[...])


def _l1_in(x1, peer, mods0, mods1, g1, w, n_x_tiles):
    B, Tt, _ = x1.shape
    tm = ROW_TILE
    n = w.shape[1]
    full = lambda a: pl.BlockSpec(a.shape, lambda b, t: (0,) * a.ndim)
    row = lambda m: pl.BlockSpec((1, tm, m), lambda b, t: (b, t, 0))
    return pl.pallas_call(
        functools.partial(_l1_in_kernel, n_x_tiles=n_x_tiles),
        grid=(B, Tt // tm),
        in_specs=[row(D), row(D), full(mods0), full(mods1), full(g1), full(w)],
        out_specs=[row(D), row(n)],
        out_shape=[jax.ShapeDtypeStruct((B, Tt, D), F32), jax.ShapeDtypeStruct((B, Tt, n), F32)],
        compiler_params=_cparams(("parallel", "parallel"), VMEM_LIMIT),
        name="l1_in",
    )(x1, peer, mods0, mods1, g1, w)


def _gla_kernel(q_ref, z_ref, v_ref, lbp_ref, o_ref, st_ref, *, layer):
    d, s = pl.program_id(0), pl.program_id(2)
    C = HG_CHUNK

    @pl.when(s == 0)
    def _():
        st_ref[...] = jnp.zeros_like(st_ref)

    fwd = d == 0
    r = lax.broadcasted_iota(jnp.int32, (C, C), 0)
    c = lax.broadcasted_iota(jnp.int32, (C, C), 1)
    tri = jnp.where(fwd, c - r, r - c) <= 0
    tri_b = jnp.where(tri, 1.0, 0.0).astype(BF16)
    lbp = lbp_ref[0]
    ex = jnp.exp(lbp - jnp.max(lbp, axis=0, keepdims=True))
    lbv = jnp.sum(ex[1:layer + 1], axis=0, keepdims=True) / jnp.sum(ex, axis=0, keepdims=True)
    for hh in range(HG_HEADS):
        sl = slice(hh * HG_DK, (hh + 1) * HG_DK)
        lb = lbv[:, sl]
        f = lb + (1.0 - lb) * jax.nn.sigmoid(z_ref[0, :, sl])
        logf = jnp.log(f)
        k = 1.0 - f
        hi = logf.astype(BF16)
        lo = (logf - hi.astype(F32)).astype(BF16)
        bc = _dot(tri_b, hi) + _dot(tri_b, lo)
        btot = jnp.where(fwd, bc[C - 1:C, :], bc[0:1, :])
        qi = (q_ref[0, :, sl] * jnp.exp(bc)).astype(BF16)
        ki = (k * jnp.exp(-bc)).astype(BF16)
        ko = (k * jnp.exp(btot - bc)).astype(BF16)
        vb = v_ref[0, :, sl].astype(BF16)
        a = lax.dot_general(qi, ki, NT_DIMS, preferred_element_type=F32)
        a = jnp.where(tri, a, 0.0).astype(BF16)
        st = st_ref[hh]
        o_ref[0, 0, :, sl] = _dot(a, vb) + lax.dot_general(qi, st.astype(BF16), NT_DIMS,
                                                            preferred_element_type=F32)
        st_ref[hh] = st * jnp.exp(btot) + lax.dot_general(vb, ko, TN_DIMS, preferred_element_type=F32)


def _gla(p, lbp, n_x_chunks, n_c_chunks, layer):
    B, Tt, _ = p.shape
    C = HG_CHUNK
    F = HG_HEADS * HG_DK
    nch = n_x_chunks + n_c_chunks

    def chunk(d, s):
        fwd = jnp.where(s < n_c_chunks, n_x_chunks + s, s - n_c_chunks)
        bwd = jnp.where(s < n_c_chunks, nch - 1 - s, nch - 1 - s)
        return jnp.where(d == 0, fwd, bwd)

    return pl.pallas_call(
        functools.partial(_gla_kernel, layer=layer),
        grid=(2, B, nch),
        in_specs=[pl.BlockSpec((1, C, F), lambda d, b, s: (b, chunk(d, s), 0)),
                  pl.BlockSpec((1, C, F), lambda d, b, s: (b, chunk(d, s), 1 + d)),
                  pl.BlockSpec((1, C, F), lambda d, b, s: (b, chunk(d, s), 3)),
                  pl.BlockSpec((1,) + lbp.shape[1:], lambda d, b, s: (d, 0, 0))],
        out_specs=pl.BlockSpec((1, 1, C, F), lambda d, b, s: (d, b, chunk(d, s), 0)),
        out_shape=jax.ShapeDtypeStruct((2, B, Tt, F), F32),
        scratch_shapes=[pltpu.VMEM((HG_HEADS, HG_DV, HG_DK), F32)],
        compiler_params=_cparams(("parallel", "parallel", "arbitrary")),
        name="gla",
    )(p, p, p, lbp)


def _l1_out_kernel(of_ref, ob_ref, g_ref, x_ref, mods_ref, ng_ref, wo_ref, g2_ref, wqt_ref,
                   x3_ref, h2_ref, qt_ref, *, n_x_tiles):
    b, t = pl.program_id(0), pl.program_id(1)
    row = _mod_row(mods_ref, b, t >= n_x_tiles)
    o = of_ref[0, 0] + ob_ref[0, 0]
    parts = []
    for hh in range(HG_HEADS):
        oh = o[:, hh * HG_DV:(hh + 1) * HG_DV]
        parts.append(oh * lax.rsqrt(jnp.mean(oh * oh, axis=-1, keepdims=True) + EPS))
    on = jnp.concatenate(parts, axis=1)
    y = (on * ng_ref[...] * _silu(g_ref[0])).astype(BF16)
    x3 = x_ref[0] + _mod(row, 2) * _dot(y, wo_ref[...])
    x3_ref[0] = x3
    _peer_q_part(x3, row, g2_ref, wqt_ref, h2_ref, qt_ref)


def _l1_out(o2, p, x2, mods1, ng, wo, g2, wqt, n_x_tiles):
    B, Tt, _ = x2.shape
    tm = ROW_TILE
    full = lambda a: pl.BlockSpec(a.shape, lambda b, t: (0,) * a.ndim)
    row = lambda n: pl.BlockSpec((1, tm, n), lambda b, t: (b, t, 0))
    out_specs, out_shape = _peer_q_outs(B, Tt)
    return pl.pallas_call(
        functools.partial(_l1_out_kernel, n_x_tiles=n_x_tiles),
        grid=(B, Tt // tm),
        in_specs=[pl.BlockSpec((1, 1, tm, D), lambda b, t: (0, b, t, 0)),
                  pl.BlockSpec((1, 1, tm, D), lambda b, t: (1, b, t, 0)),
                  pl.BlockSpec((1, tm, D), lambda b, t: (b, t, 4)),
                  row(D), full(mods1), full(ng), full(wo), full(g2), full(wqt)],
        out_specs=out_specs, out_shape=out_shape,
        compiler_params=_cparams(("parallel", "parallel"), VMEM_LIMIT),
        name="l1_out",
    )(o2, o2, p, x2, mods1, ng, wo, g2, wqt)


def _final_kernel(x_ref, peer_ref, mods_ref, g_ref, o_ref):
    b = pl.program_id(0)
    row = mods_ref[pl.ds(b, 1), :]
    o_ref[0] = _rms(x_ref[0] + _mod(row, 5) * peer_ref[0], g_ref[...])


def _final(x3, peer, mods1, g, T):
    B = x3.shape[0]
    tm = ROW_TILE
    full = lambda a: pl.BlockSpec(a.shape, lambda b, t: (0,) * a.ndim)
    row = pl.BlockSpec((1, tm, D), lambda b, t: (b, t, 0))
    return pl.pallas_call(
        _final_kernel,
        grid=(B, T // tm),
        in_specs=[row, row, full(mods1), full(g)],
        out_specs=row,
        out_shape=jax.ShapeDtypeStruct((B, T, D), F32),
        compiler_params=_cparams(("parallel", "parallel")),
        name="final",
    )(x3, peer, mods1, g)


def _rope_tables(T, Tc):
    rows = T // GRID_W
    row = jnp.repeat(jnp.arange(rows, dtype=F32), GRID_W)
    col = jnp.tile(jnp.arange(GRID_W, dtype=F32), rows)
    axis_dim = MLA_ROPE // 2
    inv = ROPE_BASE ** (-jnp.arange(0, axis_dim, 2, dtype=F32) / axis_dim)
    ang = jnp.concatenate([row[:, None] * inv, col[:, None] * inv], axis=-1)
    cos = jnp.concatenate([jnp.cos(ang), jnp.ones((Tc, axis_dim), F32)], axis=0)
    sin = jnp.concatenate([jnp.sin(ang), jnp.zeros((Tc, axis_dim), F32)], axis=0)
    Tt = T + Tc
    ct = jnp.concatenate([jnp.ones((Tt, MLA_NOPE), F32), cos, cos, jnp.ones((Tt, 32), F32)], axis=1)
    st = jnp.concatenate([jnp.zeros((Tt, MLA_NOPE), F32), -sin, sin, jnp.zeros((Tt, 32), F32)], axis=1)
    return ct, st


def _mla_weights(w_in, q_g, w_uq, kv_g, w_ukv):
    z = lambda *s: jnp.zeros(s, F32)
    c1 = MLA_Q_RANK + MLA_KV_RANK
    kr = w_in[:, c1:c1 + MLA_ROPE]
    kr_e, kr_o = kr[:, 0::2], kr[:, 1::2]
    krblk = jnp.concatenate([z(D, MLA_NOPE), kr_e, kr_o, z(D, 32)], axis=1)
    krsblk = jnp.concatenate([z(D, MLA_NOPE), kr_o, kr_e, z(D, 32)], axis=1)
    w1 = jnp.concatenate([w_in[:, :c1], krblk, krsblk], axis=1)
    w2 = w_in[:, c1 + MLA_ROPE:]
    wu = w_uq.reshape(MLA_Q_RANK, MLA_HEADS, MLA_NOPE + MLA_ROPE)
    nope, r = wu[:, :, :MLA_NOPE], wu[:, :, MLA_NOPE:]
    re, ro = r[:, :, 0::2], r[:, :, 1::2]
    zq = lambda n: z(MLA_Q_RANK, MLA_HEADS, n)
    wq = jnp.concatenate([nope, re, ro, zq(32)], axis=-1).reshape(MLA_Q_RANK, -1)
    wqs = jnp.concatenate([zq(MLA_NOPE), ro, re, zq(32)], axis=-1).reshape(MLA_Q_RANK, -1)
    wkv = w_ukv.reshape(MLA_KV_RANK, MLA_HEADS, MLA_NOPE + MLA_V)
    zk = z(MLA_KV_RANK, MLA_HEADS, HEAD_PAD - MLA_NOPE)
    wk = jnp.concatenate([wkv[:, :, :MLA_NOPE], zk], axis=-1).reshape(MLA_KV_RANK, -1)
    wv = wkv[:, :, MLA_NOPE:].reshape(MLA_KV_RANK, -1)
    lane = jnp.arange(HEAD_PAD)
    place = jnp.where((lane >= MLA_NOPE) & (lane < MLA_NOPE + MLA_ROPE), 1.0, 0.0)
    e = jnp.tile(jnp.diag(place), (1, MLA_HEADS))
    bf = lambda a: a.astype(BF16)
    return dict(w1=bf(w1), w2=bf(w2), qg=q_g.reshape(1, -1), wq=bf(wq), wqs=bf(wqs),
                kvg=kv_g.reshape(1, -1), wk=bf(wk), wv=bf(wv), e=bf(e))


def _out_weights(w_out):
    return w_out[:MLA_HEADS * MLA_V].astype(BF16), w_out[MLA_HEADS * MLA_V:].astype(BF16)


def _peer(h2t, qt, sub_keys, u, v, B):
    thr, ea, code, ebz = _route(qt, sub_keys.astype(BF16))
    out = _dense(h2t, u.astype(BF16), (0.5 * v).T.astype(BF16), thr, ea, code, ebz)
    return out.reshape(B, -1, D)


def kernel(x, c, ctx, c_ctx, ada_w, ada_b, norm1_g, norm2_g, ab_w_in, mla_q_norm_g, mla_w_uq, mla_kv_norm_g, mla_w_ukv, conv_w, conv_b, conv_ln_g, conv_ln_b, ab_w_out, hg_w_in, hg_lower_bound, hg_norm_g, hg_w_out, peer_w_q, peer_sub_keys, peer_u, peer_v, final_norm_g):
    B, T, _ = x.shape
    Tc = ctx.shape[1]
    assert B <= 2 and T % ROW_TILE == 0 and Tc % ROW_TILE == 0 and ada_w.shape[0] == 2
    n_x_tiles = T // ROW_TILE
    row2 = lambda a: a.reshape(1, -1)

    xx = jnp.concatenate([x, ctx], axis=1)
    c8 = jnp.zeros((8, D), F32).at[:B].set(c).at[2].set(c_ctx)
    mods = _mods(c8, ada_w, ada_b)

    ct, st = _rope_tables(T, Tc)
    mw = _mla_weights(ab_w_in[0], mla_q_norm_g[0], mla_w_uq[0], mla_kv_norm_g[0], mla_w_ukv[0])
    q, k, v, y = _l0_in(xx, mods[0], row2(norm1_g[0]), mw, ct, st, n_x_tiles)
    o = _attention(q, k, v, T, Tc)
    cv = _conv(y, conv_w[0], row2(conv_b[0]), row2(conv_ln_g[0]), row2(conv_ln_b[0]), n_x_tiles)
    woa, woc = _out_weights(ab_w_out[0])
    x1, h2, qt = _l0_out(o, cv, xx, mods[0], woa, woc, row2(norm2_g[0]), peer_w_q[0].T.astype(BF16), n_x_tiles)
    peer0 = _peer(h2, qt, peer_sub_keys[0], peer_u[0], peer_v[0], B)

    x2, p = _l1_in(x1, peer0, mods[0], mods[1], row2(norm1_g[1]), hg_w_in[0].astype(BF16), n_x_tiles)
    o2 = _gla(p, jnp.transpose(hg_lower_bound, (1, 0, 2)), T // HG_CHUNK, Tc // HG_CHUNK, layer=1)
    x3, h2, qt = _l1_out(o2, p, x2, mods[1], row2(hg_norm_g[0]), hg_w_out[0].astype(BF16),
                         row2(norm2_g[1]), peer_w_q[1].T.astype(BF16), n_x_tiles)
    peer1 = _peer(h2, qt, peer_sub_keys[1], peer_u[1], peer_v[1], B)
    return _final(x3, peer1, mods[1], row2(final_norm_g), T)
```

```python
import functools

import jax
import jax.numpy as jnp
from jax import lax
from jax.experimental import pallas as pl
from jax.experimental.pallas import tpu as pltpu

F32 = jnp.float32
BF16 = jnp.bfloat16

D = 1024
EPS = 1e-6
GRID_W = 64
ROPE_BASE = 10000.0
MLA_HEADS, MLA_NOPE, MLA_ROPE, MLA_V = 8, 64, 32, 64
MLA_Q_RANK, MLA_KV_RANK = 256, 128
MLA_SCALE = (MLA_NOPE + MLA_ROPE) ** -0.5
LOG2E = 1.4426950408889634
CONV_CH, CONV_WIDTH = 512, 31
HEAD_PAD = 128
HG_HEADS, HG_DK, HG_DV, HG_CHUNK = 8, 128, 128, 64
PEER_HEADS, PEER_NKEYS, PEER_HALF, PEER_TOPK = 8, 128, 128, 16
PEER_EXPERTS = PEER_NKEYS * PEER_NKEYS
INV_SQRT2 = 0.7071067811865476

ROW_TILE = 256
VMEM_LIMIT = 56 * 1024 * 1024

NT_DIMS = (((1,), (1,)), ((), ()))
TN_DIMS = (((0,), (0,)), ((), ()))


def _cparams(sem, vmem=None):
    return pltpu.CompilerParams(dimension_semantics=sem, vmem_limit_bytes=vmem)


def _dot(a, b):
    return jnp.dot(a, b, preferred_element_type=F32)


def _rms(x, g):
    return x * lax.rsqrt(jnp.mean(x * x, axis=-1, keepdims=True) + EPS) * g


def _silu(x):
    return x * jax.nn.sigmoid(x)


def _mod_row(mods_ref, b, is_ctx):
    r = jnp.where(is_ctx, 2, b)
    return mods_ref[pl.ds(r, 1), :]


def _mod(row, k):
    return row[:, k * D:(k + 1) * D]


def _mods_kernel(c_ref, w_ref, b_ref, o_ref):
    s = _silu(c_ref[...])
    o_ref[0] = jnp.dot(s, w_ref[0], preferred_element_type=F32,
                       precision=lax.Precision.HIGHEST) + b_ref[0]


def _mods(c8, ada_w, ada_b):
    depth = ada_w.shape[0]
    nb = 6 * D // 1024
    return pl.pallas_call(
        _mods_kernel,
        grid=(depth, nb),
        in_specs=[pl.BlockSpec((8, D), lambda l, j: (0, 0)),
                  pl.BlockSpec((1, D, 1024), lambda l, j: (l, 0, j)),
                  pl.BlockSpec((1, 1, 1024), lambda l, j: (l, 0, j))],
        out_specs=pl.BlockSpec((1, 8, 1024), lambda l, j: (l, 0, j)),
        out_shape=jax.ShapeDtypeStruct((depth, 8, 6 * D), F32),
        compiler_params=_cparams(("parallel", "parallel")),
        name="mods",
    )(c8, ada_w, ada_b.reshape(depth, 1, 6 * D))


def _l0_in_kernel(x_ref, mods_ref, g1_ref, w1_ref, w2_ref, qg_ref, wq_ref, wqs_ref, kvg_ref, wk_ref,
                  wv_ref, e_ref, ct_ref, st_ref, q_ref, k_ref, v_ref, y_ref, *, n_x_tiles):
    b, t = pl.program_id(0), pl.program_id(1)
    row = _mod_row(mods_ref, b, t >= n_x_tiles)
    h = _rms(x_ref[0], g1_ref[...]) * (1.0 + _mod(row, 1)) + _mod(row, 0)
    hb = h.astype(BF16)
    u1 = _dot(hb, w1_ref[...])
    u2 = _dot(hb, w2_ref[...])
    y_ref[0] = u2[:, :CONV_CH] * jax.nn.sigmoid(u2[:, CONV_CH:])
    ct, st = ct_ref[...], st_ref[...]
    cqn = _rms(u1[:, :MLA_Q_RANK], qg_ref[...]).astype(BF16)
    q = _dot(cqn, wq_ref[...])
    qs = _dot(cqn, wqs_ref[...])
    qr = [(q[:, hh * HEAD_PAD:(hh + 1) * HEAD_PAD] * ct + qs[:, hh * HEAD_PAD:(hh + 1) * HEAD_PAD] * st)
          * (MLA_SCALE * LOG2E) for hh in range(MLA_HEADS)]
    q_ref[0] = jnp.concatenate(qr, axis=1).T.astype(BF16)
    c0 = MLA_Q_RANK
    ckvn = _rms(u1[:, c0:c0 + MLA_KV_RANK], kvg_ref[...]).astype(BF16)
    c1 = c0 + MLA_KV_RANK
    kr = (u1[:, c1:c1 + HEAD_PAD] * ct + u1[:, c1 + HEAD_PAD:c1 + 2 * HEAD_PAD] * st).astype(BF16)
    k_ref[0] = (_dot(ckvn, wk_ref[...]) + _dot(kr, e_ref[...])).astype(BF16)
    v_ref[0] = _dot(ckvn, wv_ref[...]).T.astype(BF16)


def _l0_in(xx, mods0, g1, w, ct, st, n_x_tiles):
    B, Tt, _ = xx.shape
    tm = ROW_TILE
    full = lambda a: pl.BlockSpec(a.shape, lambda b, t: (0,) * a.ndim)
    hw = MLA_HEADS * HEAD_PAD
    row_out = lambda n: pl.BlockSpec((1, tm, n), lambda b, t: (b, t, 0))
    col_out = lambda n: pl.BlockSpec((1, n, tm), lambda b, t: (b, 0, t))
    vw = MLA_HEADS * MLA_V
    weights = (w["w1"], w["w2"], w["qg"], w["wq"], w["wqs"], w["kvg"], w["wk"], w["wv"], w["e"])
    return pl.pallas_call(
        functools.partial(_l0_in_kernel, n_x_tiles=n_x_tiles),
        grid=(B, Tt // tm),
        in_specs=[row_out(D), full(mods0), full(g1)] + [full(a) for a in weights] + [
            pl.BlockSpec((tm, HEAD_PAD), lambda b, t: (t, 0)),
            pl.BlockSpec((tm, HEAD_PAD), lambda b, t: (t, 0))],
        out_specs=[col_out(hw), row_out(hw), col_out(vw), row_out(CONV_CH)],
        out_shape=[jax.ShapeDtypeStruct((B, hw, Tt), BF16), jax.ShapeDtypeStruct((B, Tt, hw), BF16),
                   jax.ShapeDtypeStruct((B, vw, Tt), BF16), jax.ShapeDtypeStruct((B, Tt, CONV_CH), F32)],
        compiler_params=_cparams(("parallel", "parallel"), VMEM_LIMIT),
        name="l0_in",
    )(xx, mods0, g1, *weights, ct, st)


def _attend(qt, k_ref, vt_ref, chunks):
    tq = qt.shape[1]
    m = jnp.full((1, tq), -jnp.inf, F32)
    l = jnp.zeros((1, tq), F32)
    acc = jnp.zeros((MLA_V, tq), F32)
    for off, kc in chunks:
        s = _dot(k_ref[0, off:off + kc, :], qt)
        m_new = jnp.maximum(m, jnp.max(s, axis=0, keepdims=True))
        p = jnp.exp2(s - m_new)
        alpha = jnp.exp2(m - m_new)
        l = alpha * l + jnp.sum(p, axis=0, keepdims=True)
        acc = alpha * acc + _dot(vt_ref[0, :, off:off + kc], p.astype(BF16))
        m = m_new
    return acc / l


def _attn_kernel(qt_ref, k_ref, vt_ref, o_ref, *, kc):
    tk = k_ref.shape[1]
    o_ref[0] = _attend(qt_ref[0], k_ref, vt_ref, [(o, kc) for o in range(0, tk, kc)]).astype(BF16)


def _attn_call(qt, k, vt, tq, n_q, q0, tk, k0, kc, name):
    B = k.shape[0]
    return pl.pallas_call(
        functools.partial(_attn_kernel, kc=kc),
        grid=(B, MLA_HEADS, n_q),
        in_specs=[pl.BlockSpec((1, HEAD_PAD, tq), lambda b, h, t: (b, h, q0 + t)),
                  pl.BlockSpec((1, tk, HEAD_PAD), lambda b, h, t: (b, k0, h)),
                  pl.BlockSpec((1, MLA_V, tk), lambda b, h, t: (b, h, k0))],
        out_specs=pl.BlockSpec((1, MLA_V, tq), lambda b, h, t: (b, h, t)),
        out_shape=jax.ShapeDtypeStruct((B, MLA_HEADS * MLA_V, n_q * tq), BF16),
        compiler_params=_cparams(("parallel", "parallel", "parallel"), VMEM_LIMIT),
        name=name,
    )(qt, k, vt)


def _attention(qt, k, vt, T, t_ctx):
    Tt = k.shape[1]
    tq = 512 if T % 512 == 0 else ROW_TILE
    kc = 768 if Tt % 768 == 0 else ROW_TILE
    o_x = _attn_call(qt, k, vt, tq, T // tq, 0, Tt, 0, kc, "attn")
    o_c = _attn_call(qt, k, vt, t_ctx, 1, T // t_ctx, t_ctx, T // t_ctx, t_ctx, "attn_ctx")
    return jnp.concatenate([o_x, o_c], axis=2)


def _conv_kernel(prev_ref, cur_ref, next_ref, w_ref, b_ref, g_ref, beta_ref, o_ref, buf_ref, *, n_x_tiles):
    t, nt = pl.program_id(1), pl.num_programs(1)
    tm = cur_ref.shape[1]
    pad = 16
    pv = jnp.where((t == 0) | (t == n_x_tiles), 0.0, 1.0)
    nv = jnp.where((t == n_x_tiles - 1) | (t == nt - 1), 0.0, 1.0)
    buf_ref[0:pad, :] = prev_ref[0, tm - pad:tm, :] * pv
    buf_ref[pad:pad + tm, :] = cur_ref[0]
    buf_ref[pad + tm:2 * pad + tm, :] = next_ref[0, 0:pad, :] * nv
    acc = jnp.zeros((tm, CONV_CH), F32)
    half = CONV_WIDTH // 2
    for k in range(CONV_WIDTH):
        acc = acc + buf_ref[pad - half + k:pad - half + k + tm, :] * w_ref[k:k + 1, :]
    y = acc + b_ref[...]
    mu = jnp.mean(y, axis=-1, keepdims=True)
    yc = y - mu
    yn = yc * lax.rsqrt(jnp.mean(yc * yc, axis=-1, keepdims=True) + EPS) * g_ref[...] + beta_ref[...]
    o_ref[0] = _silu(yn).astype(BF16)


def _conv(y, conv_w, conv_b, ln_g, ln_b, n_x_tiles):
    B, Tt, _ = y.shape
    tm = ROW_TILE
    nt = Tt // tm
    full = lambda a: pl.BlockSpec(a.shape, lambda b, t: (0,) * a.ndim)
    return pl.pallas_call(
        functools.partial(_conv_kernel, n_x_tiles=n_x_tiles),
        grid=(B, nt),
        in_specs=[pl.BlockSpec((1, tm, CONV_CH), lambda b, t: (b, jnp.maximum(t - 1, 0), 0)),
                  pl.BlockSpec((1, tm, CONV_CH), lambda b, t: (b, t, 0)),
                  pl.BlockSpec((1, tm, CONV_CH), lambda b, t: (b, jnp.minimum(t + 1, nt - 1), 0)),
                  full(conv_w), full(conv_b), full(ln_g), full(ln_b)],
        out_specs=pl.BlockSpec((1, tm, CONV_CH), lambda b, t: (b, t, 0)),
        out_shape=jax.ShapeDtypeStruct((B, Tt, CONV_CH), BF16),
        scratch_shapes=[pltpu.VMEM((tm + 32, CONV_CH), F32)],
        compiler_params=_cparams(("parallel", "parallel")),
        name="conv",
    )(y, y, y, conv_w, conv_b, ln_g, ln_b)


def _peer_q_part(x, row, g2_ref, wqt_ref, h2_ref, qt_ref):
    h2t = (_rms(x, g2_ref[...]) * (1.0 + _mod(row, 4)) + _mod(row, 3)).T.astype(BF16)
    h2_ref[...] = h2t
    qt_ref[...] = _dot(wqt_ref[...], h2t)


def _l0_out_kernel(o_ref, cv_ref, x_ref, mods_ref, woa_ref, woc_ref, g2_ref, wqt_ref,
                   x1_ref, h2_ref, qt_ref, *, n_x_tiles):
    b, t = pl.program_id(0), pl.program_id(1)
    row = _mod_row(mods_ref, b, t >= n_x_tiles)
    y = _dot(o_ref[0].T, woa_ref[...]) + _dot(cv_ref[0], woc_ref[...])
    x1 = x_ref[0] + _mod(row, 2) * y
    x1_ref[0] = x1
    _peer_q_part(x1, row, g2_ref, wqt_ref, h2_ref, qt_ref)


def _peer_q_outs(B, Tt):
    tm = ROW_TILE
    nq = PEER_HEADS * 2 * PEER_HALF
    nt = Tt // tm
    specs = [pl.BlockSpec((1, tm, D), lambda b, t: (b, t, 0)),
             pl.BlockSpec((D, tm), lambda b, t: (0, b * nt + t)),
             pl.BlockSpec((nq, tm), lambda b, t: (0, b * nt + t))]
    shapes = [jax.ShapeDtypeStruct((B, Tt, D), F32), jax.ShapeDtypeStruct((D, B * Tt), BF16),
              jax.ShapeDtypeStruct((nq, B * Tt), F32)]
    return specs, shapes


def _l0_out(o, cv, xx, mods0, woa, woc, g2, wqt, n_x_tiles):
    B, Tt, _ = xx.shape
    tm = ROW_TILE
    full = lambda a: pl.BlockSpec(a.shape, lambda b, t: (0,) * a.ndim)
    row = lambda n: pl.BlockSpec((1, tm, n), lambda b, t: (b, t, 0))
    out_specs, out_shape = _peer_q_outs(B, Tt)
    return pl.pallas_call(
        functools.partial(_l0_out_kernel, n_x_tiles=n_x_tiles),
        grid=(B, Tt // tm),
        in_specs=[pl.BlockSpec((1, o.shape[1], tm), lambda b, t: (b, 0, t)), row(CONV_CH), row(D), full(mods0),
                  full(woa), full(woc), full(g2), full(wqt)],
        out_specs=out_specs, out_shape=out_shape,
        compiler_params=_cparams(("parallel", "parallel"), VMEM_LIMIT),
        name="l0_out",
    )(o, cv, xx, mods0, woa, woc, g2, wqt)


def _top16(work, sorted_ref):
    code = jnp.zeros_like(work)
    for p in range(PEER_TOPK):
        m = jnp.max(work, axis=0, keepdims=True)
        sel = work == m
        code = jnp.where(sel, float(PEER_TOPK - p), code)
        work = jnp.where(sel, -jnp.inf, work)
        sorted_ref[p:p + 1, :] = m
    return code


def _pair_word(x):
    hi = lax.shift_right_logical(pltpu.bitcast(x.astype(BF16).astype(F32), jnp.uint32), jnp.uint32(16))
    return lax.shift_left(hi, jnp.uint32(16)) | hi


def _route_kernel(qt_ref, sk_ref, thr_ref, ea_ref, code_ref, ebz_ref, as_ref, bs_ref):
    K = PEER_TOPK
    for hh in range(PEER_HEADS):
        r0 = hh * 2 * PEER_HALF
        qa = qt_ref[r0:r0 + PEER_HALF, :].astype(BF16)
        qb = qt_ref[r0 + PEER_HALF:r0 + 2 * PEER_HALF, :].astype(BF16)
        a = _dot(sk_ref[hh, 0], qa)
        b = _dot(sk_ref[hh, 1], qb)
        ra = _top16(a, as_ref)
        rb = _top16(b, bs_ref)
        av, bv = as_ref[...], bs_ref[...]
        H8 = K // 2
        q8 = lax.broadcasted_iota(jnp.int32, (H8, a.shape[1]), 0)
        c0 = av[0:1, :] + bv
        cmid = [jnp.where(q8 < K // (p + 1), av[p:p + 1, :] + bv[0:H8, :], -jnp.inf) for p in range(1, H8)]
        ctail = av[H8:K, :] + bv[0:1, :]
        work = [c0[0:H8, :], c0[H8:K, :]] + cmid + [ctail]
        tau = None
        for it in range(K):
            m = functools.reduce(jnp.maximum, work)
            tau = jnp.max(m, axis=0, keepdims=True)
            if it < K - 1:
                work = [jnp.where(w == tau, -jnp.inf, w) for w in work]
        ea_s = jnp.exp(av - av[0:1, :])
        eb_s = jnp.exp(bv - bv[0:1, :])
        thr = jnp.where(a + bv[0:1, :] >= tau, float(K), float(K + 1))
        sel_t = ctail >= tau
        z = jnp.sum(jnp.where(sel_t, ea_s[H8:K, :], 0.0), axis=0, keepdims=True) * eb_s[0:1, :]
        for p in range(H8):
            cp, ebp = (c0, eb_s) if p == 0 else (cmid[p - 1], eb_s[0:H8, :])
            selp = cp >= tau
            cnt = jnp.sum(jnp.where(selp, 1.0, 0.0), axis=0, keepdims=True)
            z = z + ea_s[p:p + 1, :] * jnp.sum(jnp.where(selp, ebp, 0.0), axis=0, keepdims=True)
            thr = jnp.where(ra == float(K - p), float(K + 1) - cnt, thr)
        thr_ref[hh] = _pair_word(thr)
        ea_ref[hh] = _pair_word(jnp.exp(a - av[0:1, :]))
        code_ref[hh] = rb.astype(BF16)
        ebz_ref[hh] = (jnp.exp(b - bv[0:1, :]) / z).astype(BF16)


def _route(qt, sk):
    nq, n = qt.shape
    tm = ROW_TILE
    spec = pl.BlockSpec((PEER_HEADS, PEER_NKEYS, tm), lambda t: (0, 0, t))
    shape = lambda dt: jax.ShapeDtypeStruct((PEER_HEADS, PEER_NKEYS, n), dt)
    return pl.pallas_call(
        _route_kernel,
        grid=(n // tm,),
        in_specs=[pl.BlockSpec((nq, tm), lambda t: (0, t)),
                  pl.BlockSpec(sk.shape, lambda t: (0, 0, 0, 0))],
        out_specs=[spec] * 4, out_shape=[shape(jnp.uint32), shape(jnp.uint32), shape(BF16), shape(BF16)],
        scratch_shapes=[pltpu.VMEM((PEER_TOPK, tm), F32), pltpu.VMEM((PEER_TOPK, tm), F32)],
        compiler_params=_cparams(("parallel",)),
        name="peer_route",
    )(qt, sk)


DENSE_TE = 512


def _dense_kernel(h2t_ref, u_ref, vt_ref, thr_ref, ea_ref, code_ref, ebz_ref, o_ref, acc_ref, s_ref, a_ref):
    s, ns = pl.program_id(1), pl.num_programs(1)
    ne = ns - 2
    te = u_ref.shape[0]
    tm = h2t_ref.shape[1]
    nib = te // PEER_NKEYS
    pairs = PEER_NKEYS // 2
    zero = jnp.zeros((), BF16)

    @pl.when(s == 0)
    def _():
        acc_ref[...] = jnp.zeros_like(acc_ref)
        s_ref[...] = jnp.zeros_like(s_ref)
        a_ref[...] = jnp.zeros_like(a_ref)

    cur = pl.multiple_of((s % 2) * te, te)
    prv = pl.multiple_of(((s + 1) % 2) * te, te)

    acc_ref[...] += _dot(vt_ref[...], a_ref[pl.ds(cur, te), :])

    sc = s_ref[pl.ds(prv, te), :]
    act = (sc * (1.0 + lax.erf(sc * INV_SQRT2))).astype(BF16)
    tile = jnp.clip(s - 1, 0, ne - 1)
    for ib in range(nib):
        i = tile * nib + ib
        w = jnp.zeros((PEER_NKEYS, tm), BF16)
        for hh in range(PEER_HEADS):
            thr = pltpu.bitcast(jnp.broadcast_to(thr_ref[hh, pl.ds(i, 1), :], (pairs, tm)), BF16)
            ea = pltpu.bitcast(jnp.broadcast_to(ea_ref[hh, pl.ds(i, 1), :], (pairs, tm)), BF16)
            w = w + jnp.where(code_ref[hh] >= thr, ebz_ref[hh] * ea, zero)
        r0 = pl.multiple_of(prv + ib * PEER_NKEYS, PEER_NKEYS)
        a_ref[pl.ds(r0, PEER_NKEYS), :] = w * act[ib * PEER_NKEYS:(ib + 1) * PEER_NKEYS]

    s_ref[pl.ds(cur, te), :] = _dot(u_ref[...], h2t_ref[...])

    @pl.when(s == ns - 1)
    def _():
        o_ref[...] = acc_ref[...].T


def _dense_tile(n):
    for tm in (512, 256, 128):
        if n % tm == 0:
            return tm
    raise ValueError(n)


def _dense(h2t, u, vt_half, thr, ea, code, ebz):
    n = h2t.shape[1]
    tm = _dense_tile(n)
    te = DENSE_TE
    ne = PEER_EXPERTS // te
    rspec = pl.BlockSpec((PEER_HEADS, PEER_NKEYS, tm), lambda t, s: (0, 0, t))
    return pl.pallas_call(
        _dense_kernel,
        grid=(n // tm, ne + 2),
        in_specs=[pl.BlockSpec((D, tm), lambda t, s: (0, t)),
                  pl.BlockSpec((te, D), lambda t, s: (jnp.minimum(s, ne - 1), 0)),
                  pl.BlockSpec((D, te), lambda t, s: (0, jnp.maximum(s - 2, 0))),
                  rspec, rspec, rspec, rspec],
        out_specs=pl.BlockSpec((tm, D), lambda t, s: (t, 0)),
        out_shape=jax.ShapeDtypeStruct((n, D), F32),
        scratch_shapes=[pltpu.VMEM((D, tm), F32), pltpu.VMEM((2 * te, tm), F32), pltpu.VMEM((2 * te, tm), BF16)],
        compiler_params=_cparams(("parallel", "arbitrary"), VMEM_LIMIT),
        name="peer_dense",
    )(h2t, u, vt_half, thr, ea, code, ebz)


def _l1_in_kernel(x1_ref, peer_ref, mods0_ref, mods1_ref, g1_ref, w_ref, x2_ref, p_ref, *, n_x_tiles):
    b, t = pl.program_id(0), pl.program_id(1)
    ctx = t >= n_x_tiles
    row0 = _mod_row(mods0_ref, b, ctx)
    row1 = _mod_row(mods1_ref, b, ctx)
    x2 = x1_ref[0] + _mod(row0, 5) * peer_ref[0]
    x2_ref[0] = x2
    h = _rms(x2, g1_ref[...]) * (1.0 + _mod(row1, 1)) + _mod(row1, 0)
    p_ref[0] = _dot(h.astype(BF16), w_ref[...])


def _l1_in(x1, peer, mods0, mods1, g1, w, n_x_tiles):
    B, Tt, _ = x1.shape
    tm = ROW_TILE
    n = w.shape[1]
    full = lambda a: pl.BlockSpec(a.shape, lambda b, t: (0,) * a.ndim)
    row = lambda m: pl.BlockSpec((1, tm, m), lambda b, t: (b, t, 0))
    return pl.pallas_call(
        functools.partial(_l1_in_kernel, n_x_tiles=n_x_tiles),
        grid=(B, Tt // tm),
        in_specs=[row(D), row(D), full(mods0), full(mods1), full(g1), full(w)],
        out_specs=[row(D), row(n)],
        out_shape=[jax.ShapeDtypeStruct((B, Tt, D), F32), jax.ShapeDtypeStruct((B, Tt, n), F32)],
        compiler_params=_cparams(("parallel", "parallel"), VMEM_LIMIT),
        name="l1_in",
    )(x1, peer, mods0, mods1, g1, w)


def _gla_dir(d, q_ref, z_ref, v_ref, lbp_ref, o_ref, st_ref, layer):
    C = HG_CHUNK
    r = lax.broadcasted_iota(jnp.int32, (C, C), 0)
    c = lax.broadcasted_iota(jnp.int32, (C, C), 1)
    tri = (c <= r) if d == 0 else (c >= r)
    tri_b = jnp.where(tri, 1.0, 0.0).astype(BF16)
    lbp = lbp_ref[d]
    ex = jnp.exp(lbp - jnp.max(lbp, axis=0, keepdims=True))
    lb = jnp.sum(ex[1:layer + 1], axis=0, keepdims=True) / jnp.sum(ex, axis=0, keepdims=True)
    f = lb + (1.0 - lb) * jax.nn.sigmoid(z_ref[0])
    logf = jnp.log(f)
    k = 1.0 - f
    hi = logf.astype(BF16)
    lo = (logf - hi.astype(F32)).astype(BF16)
    bc = _dot(tri_b, hi) + _dot(tri_b, lo)
    btot = bc[C - 1:C, :] if d == 0 else bc[0:1, :]
    qi = (q_ref[0] * jnp.exp(bc)).astype(BF16)
    ki = (k * jnp.exp(-bc)).astype(BF16)
    ko = (k * jnp.exp(btot - bc)).astype(BF16)
    vb = v_ref[0].astype(BF16)
    dec = jnp.exp(btot)
    for hh in range(HG_HEADS):
        sl = slice(hh * HG_DK, (hh + 1) * HG_DK)
        a = lax.dot_general(qi[:, sl], ki[:, sl], NT_DIMS, preferred_element_type=F32)
        a = jnp.where(tri, a, 0.0).astype(BF16)
        st = st_ref[d, hh]
        o_ref[0, :, sl] = _dot(a, vb[:, sl]) + lax.dot_general(qi[:, sl], st.astype(BF16), NT_DIMS,
                                                                preferred_element_type=F32)
        st_ref[d, hh] = st * dec[:, sl] + lax.dot_general(vb[:, sl], ko[:, sl], TN_DIMS,
                                                           preferred_element_type=F32)


def _gla_kernel(qf_ref, zf_ref, vf_ref, qb_ref, zb_ref, vb_ref, lbp_ref, of_ref, ob_ref, st_ref, *, layer):
    @pl.when(pl.program_id(1) == 0)
    def _():
        st_ref[...] = jnp.zeros_like(st_ref)

    _gla_dir(0, qf_ref, zf_ref, vf_ref, lbp_ref, of_ref, st_ref, layer)
    _gla_dir(1, qb_ref, zb_ref, vb_ref, lbp_ref, ob_ref, st_ref, layer)


def _gla(p, lbp, n_x_chunks, n_c_chunks, layer):
    B, Tt, _ = p.shape
    C = HG_CHUNK
    F = HG_HEADS * HG_DK
    nch = n_x_chunks + n_c_chunks
    cf = lambda s: jnp.where(s < n_c_chunks, n_x_chunks + s, s - n_c_chunks)
    cb = lambda s: nch - 1 - s
    blk = lambda cfn, col: pl.BlockSpec((1, C, F), lambda b, s: (b, cfn(s), col))
    return pl.pallas_call(
        functools.partial(_gla_kernel, layer=layer),
        grid=(B, nch),
        in_specs=[blk(cf, 0), blk(cf, 1), blk(cf, 3), blk(cb, 0), blk(cb, 2), blk(cb, 3),
                  pl.BlockSpec(lbp.shape, lambda b, s: (0, 0, 0))],
        out_specs=[blk(cf, 0), blk(cb, 0)],
        out_shape=[jax.ShapeDtypeStruct((B, Tt, F), F32)] * 2,
        scratch_shapes=[pltpu.VMEM((2, HG_HEADS, HG_DV, HG_DK), F32)],
        compiler_params=_cparams(("parallel", "arbitrary")),
        name="gla",
    )(p, p, p, p, p, p, lbp)


def _l1_out_kernel(of_ref, ob_ref, g_ref, x_ref, mods_ref, ng_ref, wo_ref, g2_ref, wqt_ref,
                   x3_ref, h2_ref, qt_ref, *, n_x_tiles):
    b, t = pl.program_id(0), pl.program_id(1)
    row = _mod_row(mods_ref, b, t >= n_x_tiles)
    o = of_ref[0] + ob_ref[0]
    parts = []
    for hh in range(HG_HEADS):
        oh = o[:, hh * HG_DV:(hh + 1) * HG_DV]
        parts.append(oh * lax.rsqrt(jnp.mean(oh * oh, axis=-1, keepdims=True) + EPS))
    on = jnp.concatenate(parts, axis=1)
    y = (on * ng_ref[...] * _silu(g_ref[0])).astype(BF16)
    x3 = x_ref[0] + _mod(row, 2) * _dot(y, wo_ref[...])
    x3_ref[0] = x3
    _peer_q_part(x3, row, g2_ref, wqt_ref, h2_ref, qt_ref)


def _l1_out(of, ob, p, x2, mods1, ng, wo, g2, wqt, n_x_tiles):
    B, Tt, _ = x2.shape
    tm = ROW_TILE
    full = lambda a: pl.BlockSpec(a.shape, lambda b, t: (0,) * a.ndim)
    row = lambda n: pl.BlockSpec((1, tm, n), lambda b, t: (b, t, 0))
    out_specs, out_shape = _peer_q_outs(B, Tt)
    return pl.pallas_call(
        functools.partial(_l1_out_kernel, n_x_tiles=n_x_tiles),
        grid=(B, Tt // tm),
        in_specs=[row(D), row(D), pl.BlockSpec((1, tm, D), lambda b, t: (b, t, 4)),
                  row(D), full(mods1), full(ng), full(wo), full(g2), full(wqt)],
        out_specs=out_specs, out_shape=out_shape,
        compiler_params=_cparams(("parallel", "parallel"), VMEM_LIMIT),
        name="l1_out",
    )(of, ob, p, x2, mods1, ng, wo, g2, wqt)


def _final_kernel(x_ref, peer_ref, mods_ref, g_ref, o_ref):
    b = pl.program_id(0)
    row = mods_ref[pl.ds(b, 1), :]
    o_ref[0] = _rms(x_ref[0] + _mod(row, 5) * peer_ref[0], g_ref[...])


def _final(x3, peer, mods1, g, T):
    B = x3.shape[0]
    tm = ROW_TILE
    full = lambda a: pl.BlockSpec(a.shape, lambda b, t: (0,) * a.ndim)
    row = pl.BlockSpec((1, tm, D), lambda b, t: (b, t, 0))
    return pl.pallas_call(
        _final_kernel,
        grid=(B, T // tm),
        in_specs=[row, row, full(mods1), full(g)],
        out_specs=row,
        out_shape=jax.ShapeDtypeStruct((B, T, D), F32),
        compiler_params=_cparams(("parallel", "parallel")),
        name="final",
    )(x3, peer, mods1, g)


def _rope_tables(T, Tc):
    rows = T // GRID_W
    row = jnp.repeat(jnp.arange(rows, dtype=F32), GRID_W)
    col = jnp.tile(jnp.arange(GRID_W, dtype=F32), rows)
    axis_dim = MLA_ROPE // 2
    inv = ROPE_BASE ** (-jnp.arange(0, axis_dim, 2, dtype=F32) / axis_dim)
    ang = jnp.concatenate([row[:, None] * inv, col[:, None] * inv], axis=-1)
    cos = jnp.concatenate([jnp.cos(ang), jnp.ones((Tc, axis_dim), F32)], axis=0)
    sin = jnp.concatenate([jnp.sin(ang), jnp.zeros((Tc, axis_dim), F32)], axis=0)
    Tt = T + Tc
    ct = jnp.concatenate([jnp.ones((Tt, MLA_NOPE), F32), cos, cos, jnp.ones((Tt, 32), F32)], axis=1)
    st = jnp.concatenate([jnp.zeros((Tt, MLA_NOPE), F32), -sin, sin, jnp.zeros((Tt, 32), F32)], axis=1)
    return ct, st


def _mla_weights(w_in, q_g, w_uq, kv_g, w_ukv):
    z = lambda *s: jnp.zeros(s, F32)
    c1 = MLA_Q_RANK + MLA_KV_RANK
    kr = w_in[:, c1:c1 + MLA_ROPE]
    kr_e, kr_o = kr[:, 0::2], kr[:, 1::2]
    krblk = jnp.concatenate([z(D, MLA_NOPE), kr_e, kr_o, z(D, 32)], axis=1)
    krsblk = jnp.concatenate([z(D, MLA_NOPE), kr_o, kr_e, z(D, 32)], axis=1)
    w1 = jnp.concatenate([w_in[:, :c1], krblk, krsblk], axis=1)
    w2 = w_in[:, c1 + MLA_ROPE:]
    wu = w_uq.reshape(MLA_Q_RANK, MLA_HEADS, MLA_NOPE + MLA_ROPE)
    nope, r = wu[:, :, :MLA_NOPE], wu[:, :, MLA_NOPE:]
    re, ro = r[:, :, 0::2], r[:, :, 1::2]
    zq = lambda n: z(MLA_Q_RANK, MLA_HEADS, n)
    wq = jnp.concatenate([nope, re, ro, zq(32)], axis=-1).reshape(MLA_Q_RANK, -1)
    wqs = jnp.concatenate([zq(MLA_NOPE), ro, re, zq(32)], axis=-1).reshape(MLA_Q_RANK, -1)
    wkv = w_ukv.reshape(MLA_KV_RANK, MLA_HEADS, MLA_NOPE + MLA_V)
    zk = z(MLA_KV_RANK, MLA_HEADS, HEAD_PAD - MLA_NOPE)
    wk = jnp.concatenate([wkv[:, :, :MLA_NOPE], zk], axis=-1).reshape(MLA_KV_RANK, -1)
    wv = wkv[:, :, MLA_NOPE:].reshape(MLA_KV_RANK, -1)
    lane = jnp.arange(HEAD_PAD)
    place = jnp.where((lane >= MLA_NOPE) & (lane < MLA_NOPE + MLA_ROPE), 1.0, 0.0)
    e = jnp.tile(jnp.diag(place), (1, MLA_HEADS))
    bf = lambda a: a.astype(BF16)
    return dict(w1=bf(w1), w2=bf(w2), qg=q_g.reshape(1, -1), wq=bf(wq), wqs=bf(wqs),
                kvg=kv_g.reshape(1, -1), wk=bf(wk), wv=bf(wv), e=bf(e))


def _out_weights(w_out):
    return w_out[:MLA_HEADS * MLA_V].astype(BF16), w_out[MLA_HEADS * MLA_V:].astype(BF16)


def _peer(h2t, qt, sub_keys, u, v, B):
    thr, ea, code, ebz = _route(qt, sub_keys.astype(BF16))
    out = _dense(h2t, u.astype(BF16), (0.5 * v).T.astype(BF16), thr, ea, code, ebz)
    return out.reshape(B, -1, D)


def kernel(x, c, ctx, c_ctx, ada_w, ada_b, norm1_g, norm2_g, ab_w_in, mla_q_norm_g, mla_w_uq, mla_kv_norm_g, mla_w_ukv, conv_w, conv_b, conv_ln_g, conv_ln_b, ab_w_out, hg_w_in, hg_lower_bound, hg_norm_g, hg_w_out, peer_w_q, peer_sub_keys, peer_u, peer_v, final_norm_g):
    B, T, _ = x.shape
    Tc = ctx.shape[1]
    assert B <= 2 and T % ROW_TILE == 0 and Tc % ROW_TILE == 0 and ada_w.shape[0] == 2
    n_x_tiles = T // ROW_TILE
    row2 = lambda a: a.reshape(1, -1)

    xx = jnp.concatenate([x, ctx], axis=1)
    c8 = jnp.zeros((8, D), F32).at[:B].set(c).at[2].set(c_ctx)
    mods = _mods(c8, ada_w, ada_b)

    ct, st = _rope_tables(T, Tc)
    mw = _mla_weights(ab_w_in[0], mla_q_norm_g[0], mla_w_uq[0], mla_kv_norm_g[0], mla_w_ukv[0])
    q, k, v, y = _l0_in(xx, mods[0], row2(norm1_g[0]), mw, ct, st, n_x_tiles)
    o = _attention(q, k, v, T, Tc)
    cv = _conv(y, conv_w[0], row2(conv_b[0]), row2(conv_ln_g[0]), row2(conv_ln_b[0]), n_x_tiles)
    woa, woc = _out_weights(ab_w_out[0])
    x1, h2, qt = _l0_out(o, cv, xx, mods[0], woa, woc, row2(norm2_g[0]), peer_w_q[0].T.astype(BF16), n_x_tiles)
    peer0 = _peer(h2, qt, peer_sub_keys[0], peer_u[0], peer_v[0], B)

    x2, p = _l1_in(x1, peer0, mods[0], mods[1], row2(norm1_g[1]), hg_w_in[0].astype(BF16), n_x_tiles)
    of, ob = _gla(p, jnp.transpose(hg_lower_bound, (1, 0, 2)), T // HG_CHUNK, Tc // HG_CHUNK, layer=1)
    x3, h2, qt = _l1_out(of, ob, p, x2, mods[1], row2(hg_norm_g[0]), hg_w_out[0].astype(BF16),
                         row2(norm2_g[1]), peer_w_q[1].T.astype(BF16), n_x_tiles)
    peer1 = _peer(h2, qt, peer_sub_keys[1], peer_u[1], peer_v[1], B)
    return _final(x3, peer1, mods[1], row2(final_norm_g), T)
```

```python
import functools

import jax
import jax.numpy as jnp
from jax import lax
from jax.experimental import pallas as pl
from jax.experimental.pallas import tpu as pltpu

F32 = jnp.float32
BF16 = jnp.bfloat16

D = 1024
EPS = 1e-6
GRID_W = 64
ROPE_BASE = 10000.0
MLA_HEADS, MLA_NOPE, MLA_ROPE, MLA_V = 8, 64, 32, 64
MLA_Q_RANK, MLA_KV_RANK = 256, 128
MLA_SCALE = (MLA_NOPE + MLA_ROPE) ** -0.5
LOG2E = 1.4426950408889634
CONV_CH, CONV_WIDTH = 512, 31
HEAD_PAD = 128
HG_HEADS, HG_DK, HG_DV, HG_CHUNK = 8, 128, 128, 64
PEER_HEADS, PEER_NKEYS, PEER_HALF, PEER_TOPK = 8, 128, 128, 16
PEER_EXPERTS = PEER_NKEYS * PEER_NKEYS
INV_SQRT2 = 0.7071067811865476

ROW_TILE = 256
VMEM_LIMIT = 56 * 1024 * 1024

NT_DIMS = (((1,), (1,)), ((), ()))
TN_DIMS = (((0,), (0,)), ((), ()))


def _cparams(sem, vmem=None):
    return pltpu.CompilerParams(dimension_semantics=sem, vmem_limit_bytes=vmem)


def _dot(a, b):
    return jnp.dot(a, b, preferred_element_type=F32)


def _rms(x, g):
    return x * lax.rsqrt(jnp.mean(x * x, axis=-1, keepdims=True) + EPS) * g


def _silu(x):
    return x * jax.nn.sigmoid(x)


def _mod_row(mods_ref, b, is_ctx):
    r = jnp.where(is_ctx, 2, b)
    return mods_ref[pl.ds(r, 1), :]


def _mod(row, k):
    return row[:, k * D:(k + 1) * D]


def _mods_kernel(c_ref, w_ref, b_ref, o_ref):
    s = _silu(c_ref[...])
    o_ref[0] = jnp.dot(s, w_ref[0], preferred_element_type=F32,
                       precision=lax.Precision.HIGHEST) + b_ref[0]


def _mods(c8, ada_w, ada_b):
    depth = ada_w.shape[0]
    nb = 6 * D // 1024
    return pl.pallas_call(
        _mods_kernel,
        grid=(depth, nb),
        in_specs=[pl.BlockSpec((8, D), lambda l, j: (0, 0)),
                  pl.BlockSpec((1, D, 1024), lambda l, j: (l, 0, j)),
                  pl.BlockSpec((1, 1, 1024), lambda l, j: (l, 0, j))],
        out_specs=pl.BlockSpec((1, 8, 1024), lambda l, j: (l, 0, j)),
        out_shape=jax.ShapeDtypeStruct((depth, 8, 6 * D), F32),
        compiler_params=_cparams(("parallel", "parallel")),
        name="mods",
    )(c8, ada_w, ada_b.reshape(depth, 1, 6 * D))


def _l0_in_kernel(x_ref, mods_ref, g1_ref, w1_ref, w2_ref, qg_ref, wq_ref, wqs_ref, kvg_ref, wk_ref,
                  wv_ref, e_ref, ct_ref, st_ref, q_ref, k_ref, v_ref, y_ref, *, n_x_tiles):
    b, t = pl.program_id(0), pl.program_id(1)
    row = _mod_row(mods_ref, b, t >= n_x_tiles)
    h = _rms(x_ref[0], g1_ref[...]) * (1.0 + _mod(row, 1)) + _mod(row, 0)
    hb = h.astype(BF16)
    u1 = _dot(hb, w1_ref[...])
    u2 = _dot(hb, w2_ref[...])
    y_ref[0] = u2[:, :CONV_CH] * jax.nn.sigmoid(u2[:, CONV_CH:])
    ct, st = ct_ref[...], st_ref[...]
    cqn = _rms(u1[:, :MLA_Q_RANK], qg_ref[...]).astype(BF16)
    q = _dot(cqn, wq_ref[...])
    qs = _dot(cqn, wqs_ref[...])
    qr = [(q[:, hh * HEAD_PAD:(hh + 1) * HEAD_PAD] * ct + qs[:, hh * HEAD_PAD:(hh + 1) * HEAD_PAD] * st)
          * (MLA_SCALE * LOG2E) for hh in range(MLA_HEADS)]
    q_ref[0] = jnp.concatenate(qr, axis=1).T.astype(BF16)
    c0 = MLA_Q_RANK
    ckvn = _rms(u1[:, c0:c0 + MLA_KV_RANK], kvg_ref[...]).astype(BF16)
    c1 = c0 + MLA_KV_RANK
    kr = (u1[:, c1:c1 + HEAD_PAD] * ct + u1[:, c1 + HEAD_PAD:c1 + 2 * HEAD_PAD] * st).astype(BF16)
    k_ref[0] = (_dot(ckvn, wk_ref[...]) + _dot(kr, e_ref[...])).astype(BF16)
    v_ref[0] = _dot(ckvn, wv_ref[...]).T.astype(BF16)


def _l0_in(xx, mods0, g1, w, ct, st, n_x_tiles):
    B, Tt, _ = xx.shape
    tm = ROW_TILE
    full = lambda a: pl.BlockSpec(a.shape, lambda b, t: (0,) * a.ndim)
    hw = MLA_HEADS * HEAD_PAD
    row_out = lambda n: pl.BlockSpec((1, tm, n), lambda b, t: (b, t, 0))
    col_out = lambda n: pl.BlockSpec((1, n, tm), lambda b, t: (b, 0, t))
    vw = MLA_HEADS * MLA_V
    weights = (w["w1"], w["w2"], w["qg"], w["wq"], w["wqs"], w["kvg"], w["wk"], w["wv"], w["e"])
    return pl.pallas_call(
        functools.partial(_l0_in_kernel, n_x_tiles=n_x_tiles),
        grid=(B, Tt // tm),
        in_specs=[row_out(D), full(mods0), full(g1)] + [full(a) for a in weights] + [
            pl.BlockSpec((tm, HEAD_PAD), lambda b, t: (t, 0)),
            pl.BlockSpec((tm, HEAD_PAD), lambda b, t: (t, 0))],
        out_specs=[col_out(hw), row_out(hw), col_out(vw), row_out(CONV_CH)],
        out_shape=[jax.ShapeDtypeStruct((B, hw, Tt), BF16), jax.ShapeDtypeStruct((B, Tt, hw), BF16),
                   jax.ShapeDtypeStruct((B, vw, Tt), BF16), jax.ShapeDtypeStruct((B, Tt, CONV_CH), F32)],
        compiler_params=_cparams(("parallel", "parallel"), VMEM_LIMIT),
        name="l0_in",
    )(xx, mods0, g1, *weights, ct, st)


def _attend(qt, k_ref, vt_ref, chunks):
    tq = qt.shape[1]
    m = jnp.full((1, tq), -jnp.inf, F32)
    l = jnp.zeros((1, tq), F32)
    acc = jnp.zeros((MLA_V, tq), F32)
    for off, kc in chunks:
        s = _dot(k_ref[0, off:off + kc, :], qt)
        m_new = jnp.maximum(m, jnp.max(s, axis=0, keepdims=True))
        p = jnp.exp2(s - m_new)
        alpha = jnp.exp2(m - m_new)
        l = alpha * l + jnp.sum(p, axis=0, keepdims=True)
        acc = alpha * acc + _dot(vt_ref[0, :, off:off + kc], p.astype(BF16))
        m = m_new
    return acc / l


ATTN_TQ = 512
ATTN_KC = 8448


def _attn_kernel(qt_ref, k_ref, vt_ref, o_ref, *, kc):
    tk = k_ref.shape[1]
    o_ref[0] = _attend(qt_ref[0], k_ref, vt_ref, [(o, kc) for o in range(0, tk, kc)]).astype(BF16)


def _attn_call(qt, k, vt, tq, n_q, q0, tk, k0, kc, name):
    B = k.shape[0]
    return pl.pallas_call(
        functools.partial(_attn_kernel, kc=kc),
        grid=(B, MLA_HEADS, n_q),
        in_specs=[pl.BlockSpec((1, HEAD_PAD, tq), lambda b, h, t: (b, h, q0 + t)),
                  pl.BlockSpec((1, tk, HEAD_PAD), lambda b, h, t: (b, k0, h)),
                  pl.BlockSpec((1, MLA_V, tk), lambda b, h, t: (b, h, k0))],
        out_specs=pl.BlockSpec((1, MLA_V, tq), lambda b, h, t: (b, h, t)),
        out_shape=jax.ShapeDtypeStruct((B, MLA_HEADS * MLA_V, n_q * tq), BF16),
        compiler_params=_cparams(("parallel", "parallel", "parallel"), VMEM_LIMIT),
        name=name,
    )(qt, k, vt)


def _attention(qt, k, vt, T, t_ctx):
    Tt = k.shape[1]
    tq = ATTN_TQ if T % ATTN_TQ == 0 else ROW_TILE
    kc = ATTN_KC if Tt % ATTN_KC == 0 else ROW_TILE
    o_x = _attn_call(qt, k, vt, tq, T // tq, 0, Tt, 0, kc, "attn")
    o_c = _attn_call(qt, k, vt, t_ctx, 1, T // t_ctx, t_ctx, T // t_ctx, t_ctx, "attn_ctx")
    return jnp.concatenate([o_x, o_c], axis=2)


def _conv_kernel(prev_ref, cur_ref, next_ref, w_ref, b_ref, g_ref, beta_ref, o_ref, buf_ref, *, n_x_tiles):
    t, nt = pl.program_id(1), pl.num_programs(1)
    tm = cur_ref.shape[1]
    pad = 16
    pv = jnp.where((t == 0) | (t == n_x_tiles), 0.0, 1.0)
    nv = jnp.where((t == n_x_tiles - 1) | (t == nt - 1), 0.0, 1.0)
    buf_ref[0:pad, :] = prev_ref[0, tm - pad:tm, :] * pv
    buf_ref[pad:pad + tm, :] = cur_ref[0]
    buf_ref[pad + tm:2 * pad + tm, :] = next_ref[0, 0:pad, :] * nv
    acc = jnp.zeros((tm, CONV_CH), F32)
    half = CONV_WIDTH // 2
    for k in range(CONV_WIDTH):
        acc = acc + buf_ref[pad - half + k:pad - half + k + tm, :] * w_ref[k:k + 1, :]
    y = acc + b_ref[...]
    mu = jnp.mean(y, axis=-1, keepdims=True)
    yc = y - mu
    yn = yc * lax.rsqrt(jnp.mean(yc * yc, axis=-1, keepdims=True) + EPS) * g_ref[...] + beta_ref[...]
    o_ref[0] = _silu(yn).astype(BF16)


def _conv(y, conv_w, conv_b, ln_g, ln_b, n_x_tiles):
    B, Tt, _ = y.shape
    tm = ROW_TILE
    nt = Tt // tm
    full = lambda a: pl.BlockSpec(a.shape, lambda b, t: (0,) * a.ndim)
    return pl.pallas_call(
        functools.partial(_conv_kernel, n_x_tiles=n_x_tiles),
        grid=(B, nt),
        in_specs=[pl.BlockSpec((1, tm, CONV_CH), lambda b, t: (b, jnp.maximum(t - 1, 0), 0)),
                  pl.BlockSpec((1, tm, CONV_CH), lambda b, t: (b, t, 0)),
                  pl.BlockSpec((1, tm, CONV_CH), lambda b, t: (b, jnp.minimum(t + 1, nt - 1), 0)),
                  full(conv_w), full(conv_b), full(ln_g), full(ln_b)],
        out_specs=pl.BlockSpec((1, tm, CONV_CH), lambda b, t: (b, t, 0)),
        out_shape=jax.ShapeDtypeStruct((B, Tt, CONV_CH), BF16),
        scratch_shapes=[pltpu.VMEM((tm + 32, CONV_CH), F32)],
        compiler_params=_cparams(("parallel", "parallel")),
        name="conv",
    )(y, y, y, conv_w, conv_b, ln_g, ln_b)


def _peer_q_part(x, row, g2_ref, wqt_ref, h2_ref, qt_ref):
    h2t = (_rms(x, g2_ref[...]) * (1.0 + _mod(row, 4)) + _mod(row, 3)).T.astype(BF16)
    h2_ref[...] = h2t
    qt_ref[...] = _dot(wqt_ref[...], h2t)


def _l0_out_kernel(o_ref, cv_ref, x_ref, mods_ref, woa_ref, woc_ref, g2_ref, wqt_ref,
                   x1_ref, h2_ref, qt_ref, *, n_x_tiles):
    b, t = pl.program_id(0), pl.program_id(1)
    row = _mod_row(mods_ref, b, t >= n_x_tiles)
    y = _dot(o_ref[0].T, woa_ref[...]) + _dot(cv_ref[0], woc_ref[...])
    x1 = x_ref[0] + _mod(row, 2) * y
    x1_ref[0] = x1
    _peer_q_part(x1, row, g2_ref, wqt_ref, h2_ref, qt_ref)


def _peer_q_outs(B, Tt):
    tm = ROW_TILE
    nq = PEER_HEADS * 2 * PEER_HALF
    nt = Tt // tm
    specs = [pl.BlockSpec((1, tm, D), lambda b, t: (b, t, 0)),
             pl.BlockSpec((D, tm), lambda b, t: (0, b * nt + t)),
             pl.BlockSpec((nq, tm), lambda b, t: (0, b * nt + t))]
    shapes = [jax.ShapeDtypeStruct((B, Tt, D), F32), jax.ShapeDtypeStruct((D, B * Tt), BF16),
              jax.ShapeDtypeStruct((nq, B * Tt), F32)]
    return specs, shapes


def _l0_out(o, cv, xx, mods0, woa, woc, g2, wqt, n_x_tiles):
    B, Tt, _ = xx.shape
    tm = ROW_TILE
    full = lambda a: pl.BlockSpec(a.shape, lambda b, t: (0,) * a.ndim)
    row = lambda n: pl.BlockSpec((1, tm, n), lambda b, t: (b, t, 0))
    out_specs, out_shape = _peer_q_outs(B, Tt)
    return pl.pallas_call(
        functools.partial(_l0_out_kernel, n_x_tiles=n_x_tiles),
        grid=(B, Tt // tm),
        in_specs=[pl.BlockSpec((1, o.shape[1], tm), lambda b, t: (b, 0, t)), row(CONV_CH), row(D), full(mods0),
                  full(woa), full(woc), full(g2), full(wqt)],
        out_specs=out_specs, out_shape=out_shape,
        compiler_params=_cparams(("parallel", "parallel"), VMEM_LIMIT),
        name="l0_out",
    )(o, cv, xx, mods0, woa, woc, g2, wqt)


def _top16(work, sorted_ref):
    code = jnp.zeros_like(work)
    for p in range(PEER_TOPK):
        m = jnp.max(work, axis=0, keepdims=True)
        sel = work == m
        code = jnp.where(sel, float(PEER_TOPK - p), code)
        work = jnp.where(sel, -jnp.inf, work)
        sorted_ref[p:p + 1, :] = m
    return code


def _pair_word(x):
    hi = lax.shift_right_logical(pltpu.bitcast(x.astype(BF16).astype(F32), jnp.uint32), jnp.uint32(16))
    return lax.shift_left(hi, jnp.uint32(16)) | hi


def _route_kernel(qt_ref, sk_ref, thr_ref, ea_ref, code_ref, ebz_ref, as_ref, bs_ref):
    K = PEER_TOPK
    for hh in range(PEER_HEADS):
        r0 = hh * 2 * PEER_HALF
        qa = qt_ref[r0:r0 + PEER_HALF, :].astype(BF16)
        qb = qt_ref[r0 + PEER_HALF:r0 + 2 * PEER_HALF, :].astype(BF16)
        a = _dot(sk_ref[hh, 0], qa)
        b = _dot(sk_ref[hh, 1], qb)
        ra = _top16(a, as_ref)
        rb = _top16(b, bs_ref)
        av, bv = as_ref[...], bs_ref[...]
        H8 = K // 2
        q8 = lax.broadcasted_iota(jnp.int32, (H8, a.shape[1]), 0)
        c0 = av[0:1, :] + bv
        cmid = [jnp.where(q8 < K // (p + 1), av[p:p + 1, :] + bv[0:H8, :], -jnp.inf) for p in range(1, H8)]
        ctail = av[H8:K, :] + bv[0:1, :]
        work = [c0[0:H8, :], c0[H8:K, :]] + cmid + [ctail]
        tau = None
        for it in range(K):
            m = functools.reduce(jnp.maximum, work)
            tau = jnp.max(m, axis=0, keepdims=True)
            if it < K - 1:
                work = [jnp.where(w == tau, -jnp.inf, w) for w in work]
        ea_s = jnp.exp(av - av[0:1, :])
        eb_s = jnp.exp(bv - bv[0:1, :])
        thr = jnp.where(a + bv[0:1, :] >= tau, float(K), float(K + 1))
        sel_t = ctail >= tau
        z = jnp.sum(jnp.where(sel_t, ea_s[H8:K, :], 0.0), axis=0, keepdims=True) * eb_s[0:1, :]
        for p in range(H8):
            cp, ebp = (c0, eb_s) if p == 0 else (cmid[p - 1], eb_s[0:H8, :])
            selp = cp >= tau
            cnt = jnp.sum(jnp.where(selp, 1.0, 0.0), axis=0, keepdims=True)
            z = z + ea_s[p:p + 1, :] * jnp.sum(jnp.where(selp, ebp, 0.0), axis=0, keepdims=True)
            thr = jnp.where(ra == float(K - p), float(K + 1) - cnt, thr)
        thr_ref[hh] = _pair_word(thr)
        ea_ref[hh] = _pair_word(jnp.exp(a - av[0:1, :]))
        code_ref[hh] = rb.astype(BF16)
        ebz_ref[hh] = (jnp.exp(b - bv[0:1, :]) / z).astype(BF16)


def _route(qt, sk):
    nq, n = qt.shape
    tm = ROW_TILE
    spec = pl.BlockSpec((PEER_HEADS, PEER_NKEYS, tm), lambda t: (0, 0, t))
    shape = lambda dt: jax.ShapeDtypeStruct((PEER_HEADS, PEER_NKEYS, n), dt)
    return pl.pallas_call(
        _route_kernel,
        grid=(n // tm,),
        in_specs=[pl.BlockSpec((nq, tm), lambda t: (0, t)),
                  pl.BlockSpec(sk.shape, lambda t: (0, 0, 0, 0))],
        out_specs=[spec] * 4, out_shape=[shape(jnp.uint32), shape(jnp.uint32), shape(BF16), shape(BF16)],
        scratch_shapes=[pltpu.VMEM((PEER_TOPK, tm), F32), pltpu.VMEM((PEER_TOPK, tm), F32)],
        compiler_params=_cparams(("parallel",)),
        name="peer_route",
    )(qt, sk)


DENSE_TE = 512


def _dense_kernel(h2t_ref, u_ref, vt_ref, thr_ref, ea_ref, code_ref, ebz_ref, o_ref, acc_ref, s_ref, a_ref):
    s, ns = pl.program_id(1), pl.num_programs(1)
    ne = ns - 2
    te = u_ref.shape[0]
    tm = h2t_ref.shape[1]
    nib = te // PEER_NKEYS
    pairs = PEER_NKEYS // 2
    zero = jnp.zeros((), BF16)

    @pl.when(s == 0)
    def _():
        acc_ref[...] = jnp.zeros_like(acc_ref)
        s_ref[...] = jnp.zeros_like(s_ref)
        a_ref[...] = jnp.zeros_like(a_ref)

    cur = pl.multiple_of((s % 2) * te, te)
    prv = pl.multiple_of(((s + 1) % 2) * te, te)

    acc_ref[...] += _dot(vt_ref[0], a_ref[pl.ds(cur, te), :])

    sc = s_ref[pl.ds(prv, te), :]
    act = (sc * (1.0 + lax.erf(sc * INV_SQRT2))).astype(BF16)
    tile = jnp.clip(s - 1, 0, ne - 1)
    for ib in range(nib):
        i = tile * nib + ib
        w = jnp.zeros((PEER_NKEYS, tm), BF16)
        for hh in range(PEER_HEADS):
            thr = pltpu.bitcast(jnp.broadcast_to(thr_ref[hh, pl.ds(i, 1), :], (pairs, tm)), BF16)
            ea = pltpu.bitcast(jnp.broadcast_to(ea_ref[hh, pl.ds(i, 1), :], (pairs, tm)), BF16)
            w = w + jnp.where(code_ref[hh] >= thr, ebz_ref[hh] * ea, zero)
        r0 = pl.multiple_of(prv + ib * PEER_NKEYS, PEER_NKEYS)
        a_ref[pl.ds(r0, PEER_NKEYS), :] = w * act[ib * PEER_NKEYS:(ib + 1) * PEER_NKEYS]

    s_ref[pl.ds(cur, te), :] = _dot(u_ref[...], h2t_ref[...])

    @pl.when(s == ns - 1)
    def _():
        o_ref[...] = acc_ref[...].T


def _dense_tile(n):
    for tm in (512, 256, 128):
        if n % tm == 0:
            return tm
    raise ValueError(n)


def _dense(h2t, u, v, thr, ea, code, ebz):
    n = h2t.shape[1]
    tm = _dense_tile(n)
    te = DENSE_TE
    ne = PEER_EXPERTS // te
    vt = jnp.transpose((0.5 * v).reshape(ne, te, D), (0, 2, 1)).astype(BF16)
    rspec = pl.BlockSpec((PEER_HEADS, PEER_NKEYS, tm), lambda t, s: (0, 0, t))
    return pl.pallas_call(
        _dense_kernel,
        grid=(n // tm, ne + 2),
        in_specs=[pl.BlockSpec((D, tm), lambda t, s: (0, t)),
                  pl.BlockSpec((te, D), lambda t, s: (jnp.minimum(s, ne - 1), 0)),
                  pl.BlockSpec((1, D, te), lambda t, s: (jnp.maximum(s - 2, 0), 0, 0)),
                  rspec, rspec, rspec, rspec],
        out_specs=pl.BlockSpec((tm, D), lambda t, s: (t, 0)),
        out_shape=jax.ShapeDtypeStruct((n, D), F32),
        scratch_shapes=[pltpu.VMEM((D, tm), F32), pltpu.VMEM((2 * te, tm), F32), pltpu.VMEM((2 * te, tm), BF16)],
        compiler_params=_cparams(("parallel", "arbitrary"), VMEM_LIMIT),
        name="peer_dense",
    )(h2t, u.astype(BF16), vt, thr, ea, code, ebz)


FLAT_TE = 1024
FLAT_SUB = 512


def _dense_flat_kernel(h2t_ref, u_ref, vt_ref, thr_ref, ea_ref, code_ref, ebz_ref, o_ref, acc_ref):
    e, ne = pl.program_id(1), pl.num_programs(1)
    te = u_ref.shape[0]
    tm = h2t_ref.shape[1]
    nib = te // PEER_NKEYS
    pairs = PEER_NKEYS // 2

    @pl.when(e == 0)
    def _():
        acc_ref[...] = jnp.zeros_like(acc_ref)

    h2t = h2t_ref[...]
    zero = jnp.zeros((), BF16)
    for c in range(te // FLAT_SUB):
        rows = slice(c * FLAT_SUB, (c + 1) * FLAT_SUB)
        sc = _dot(u_ref[rows, :], h2t)
        act = (sc * (1.0 + lax.erf(sc * INV_SQRT2))).astype(BF16)
        parts = []
        for ib in range(FLAT_SUB // PEER_NKEYS):
            i = e * nib + c * (FLAT_SUB // PEER_NKEYS) + ib
            w = jnp.zeros((PEER_NKEYS, tm), BF16)
            for hh in range(PEER_HEADS):
                thr = pltpu.bitcast(jnp.broadcast_to(thr_ref[hh, pl.ds(i, 1), :], (pairs, tm)), BF16)
                ea = pltpu.bitcast(jnp.broadcast_to(ea_ref[hh, pl.ds(i, 1), :], (pairs, tm)), BF16)
                w = w + jnp.where(code_ref[hh] >= thr, ebz_ref[hh] * ea, zero)
            parts.append(w * act[ib * PEER_NKEYS:(ib + 1) * PEER_NKEYS])
        acc_ref[...] += _dot(vt_ref[0, :, rows], jnp.concatenate(parts, axis=0))

    @pl.when(e == ne - 1)
    def _():
        o_ref[...] = acc_ref[...].T


def _dense_flat(h2t, u, v, thr, ea, code, ebz):
    n = h2t.shape[1]
    tm = _dense_tile(n)
    te = FLAT_TE
    ne = PEER_EXPERTS // te
    vt = jnp.transpose((0.5 * v).reshape(ne, te, D), (0, 2, 1)).astype(BF16)
    rspec = pl.BlockSpec((PEER_HEADS, PEER_NKEYS, tm), lambda t, e: (0, 0, t))
    return pl.pallas_call(
        _dense_flat_kernel,
        grid=(n // tm, ne),
        in_specs=[pl.BlockSpec((D, tm), lambda t, e: (0, t)),
                  pl.BlockSpec((te, D), lambda t, e: (e, 0)),
                  pl.BlockSpec((1, D, te), lambda t, e: (e, 0, 0)),
                  rspec, rspec, rspec, rspec],
        out_specs=pl.BlockSpec((tm, D), lambda t, e: (t, 0)),
        out_shape=jax.ShapeDtypeStruct((n, D), F32),
        scratch_shapes=[pltpu.VMEM((D, tm), F32)],
        compiler_params=_cparams(("parallel", "arbitrary"), VMEM_LIMIT),
        name="peer_dense_flat",
    )(h2t, u.astype(BF16), vt, thr, ea, code, ebz)


def _l1_in_kernel(x1_ref, peer_ref, mods0_ref, mods1_ref, g1_ref, w_ref, x2_ref, p_ref, *, n_x_tiles):
    b, t = pl.program_id(0), pl.program_id(1)
    ctx = t >= n_x_tiles
    row0 = _mod_row(mods0_ref, b, ctx)
    row1 = _mod_row(mods1_ref, b, ctx)
    x2 = x1_ref[0] + _mod(row0, 5) * peer_ref[0]
    x2_ref[0] = x2
    h = _rms(x2, g1_ref[...]) * (1.0 + _mod(row1, 1)) + _mod(row1, 0)
    p_ref[0] = _dot(h.astype(BF16), w_ref[...])


def _l1_in(x1, peer, mods0, mods1, g1, w, n_x_tiles):
    B, Tt, _ = x1.shape
    tm = ROW_TILE
    n = w.shape[1]
    full = lambda a: pl.BlockSpec(a.shape, lambda b, t: (0,) * a.ndim)
    row = lambda m: pl.BlockSpec((1, tm, m), lambda b, t: (b, t, 0))
    return pl.pallas_call(
        functools.partial(_l1_in_kernel, n_x_tiles=n_x_tiles),
        grid=(B, Tt // tm),
        in_specs=[row(D), row(D), full(mods0), full(mods1), full(g1), full(w)],
        out_specs=[row(D), row(n)],
        out_shape=[jax.ShapeDtypeStruct((B, Tt, D), F32), jax.ShapeDtypeStruct((B, Tt, n), F32)],
        compiler_params=_cparams(("parallel", "parallel"), VMEM_LIMIT),
        name="l1_in",
    )(x1, peer, mods0, mods1, g1, w)


def _gla_dir(d, q_ref, z_ref, v_ref, lbp_ref, o_ref, st_ref, layer):
    C = HG_CHUNK
    r = lax.broadcasted_iota(jnp.int32, (C, C), 0)
    c = lax.broadcasted_iota(jnp.int32, (C, C), 1)
    tri = (c <= r) if d == 0 else (c >= r)
    tri_b = jnp.where(tri, 1.0, 0.0).astype(BF16)
    lbp = lbp_ref[d]
    ex = jnp.exp(lbp - jnp.max(lbp, axis=0, keepdims=True))
    lb = jnp.sum(ex[1:layer + 1], axis=0, keepdims=True) / jnp.sum(ex, axis=0, keepdims=True)
    f = lb + (1.0 - lb) * jax.nn.sigmoid(z_ref[0])
    logf = jnp.log(f)
    k = 1.0 - f
    hi = logf.astype(BF16)
    lo = (logf - hi.astype(F32)).astype(BF16)
    bc = _dot(tri_b, hi) + _dot(tri_b, lo)
    btot = bc[C - 1:C, :] if d == 0 else bc[0:1, :]
    qi = (q_ref[0] * jnp.exp(bc)).astype(BF16)
    ki = (k * jnp.exp(-bc)).astype(BF16)
    ko = (k * jnp.exp(btot - bc)).astype(BF16)
    vb = v_ref[0].astype(BF16)
    dec = jnp.exp(btot)
    for hh in range(HG_HEADS):
        sl = slice(hh * HG_DK, (hh + 1) * HG_DK)
        a = lax.dot_general(qi[:, sl], ki[:, sl], NT_DIMS, preferred_element_type=F32)
        a = jnp.where(tri, a, 0.0).astype(BF16)
        st = st_ref[d, hh]
        o_ref[0, :, sl] = _dot(a, vb[:, sl]) + lax.dot_general(qi[:, sl], st.astype(BF16), NT_DIMS,
                                                                preferred_element_type=F32)
        st_ref[d, hh] = st * dec[:, sl] + lax.dot_general(vb[:, sl], ko[:, sl], TN_DIMS,
                                                           preferred_element_type=F32)


def _gla_kernel(qf_ref, zf_ref, vf_ref, qb_ref, zb_ref, vb_ref, lbp_ref, of_ref, ob_ref, st_ref, *, layer):
    @pl.when(pl.program_id(1) == 0)
    def _():
        st_ref[...] = jnp.zeros_like(st_ref)

    _gla_dir(0, qf_ref, zf_ref, vf_ref, lbp_ref, of_ref, st_ref, layer)
    _gla_dir(1, qb_ref, zb_ref, vb_ref, lbp_ref, ob_ref, st_ref, layer)


def _gla(p, lbp, n_x_chunks, n_c_chunks, layer):
    B, Tt, _ = p.shape
    C = HG_CHUNK
    F = HG_HEADS * HG_DK
    nch = n_x_chunks + n_c_chunks
    cf = lambda s: jnp.where(s < n_c_chunks, n_x_chunks + s, s - n_c_chunks)
    cb = lambda s: nch - 1 - s
    blk = lambda cfn, col: pl.BlockSpec((1, C, F), lambda b, s: (b, cfn(s), col))
    return pl.pallas_call(
        functools.partial(_gla_kernel, layer=layer),
        grid=(B, nch),
        in_specs=[blk(cf, 0), blk(cf, 1), blk(cf, 3), blk(cb, 0), blk(cb, 2), blk(cb, 3),
                  pl.BlockSpec(lbp.shape, lambda b, s: (0, 0, 0))],
        out_specs=[blk(cf, 0), blk(cb, 0)],
        out_shape=[jax.ShapeDtypeStruct((B, Tt, F), F32)] * 2,
        scratch_shapes=[pltpu.VMEM((2, HG_HEADS, HG_DV, HG_DK), F32)],
        compiler_params=_cparams(("parallel", "arbitrary")),
        name="gla",
    )(p, p, p, p, p, p, lbp)


def _l1_out_kernel(of_ref, ob_ref, g_ref, x_ref, mods_ref, ng_ref, wo_ref, g2_ref, wqt_ref,
                   x3_ref, h2_ref, qt_ref, *, n_x_tiles):
    b, t = pl.program_id(0), pl.program_id(1)
    row = _mod_row(mods_ref, b, t >= n_x_tiles)
    o = of_ref[0] + ob_ref[0]
    parts = []
    for hh in range(HG_HEADS):
        oh = o[:, hh * HG_DV:(hh + 1) * HG_DV]
        parts.append(oh * lax.rsqrt(jnp.mean(oh * oh, axis=-1, keepdims=True) + EPS))
    on = jnp.concatenate(parts, axis=1)
    y = (on * ng_ref[...] * _silu(g_ref[0])).astype(BF16)
    x3 = x_ref[0] + _mod(row, 2) * _dot(y, wo_ref[...])
    x3_ref[0] = x3
    _peer_q_part(x3, row, g2_ref, wqt_ref, h2_ref, qt_ref)


def _l1_out(of, ob, p, x2, mods1, ng, wo, g2, wqt, n_x_tiles):
    B, Tt, _ = x2.shape
    tm = ROW_TILE
    full = lambda a: pl.BlockSpec(a.shape, lambda b, t: (0,) * a.ndim)
    row = lambda n: pl.BlockSpec((1, tm, n), lambda b, t: (b, t, 0))
    out_specs, out_shape = _peer_q_outs(B, Tt)
    return pl.pallas_call(
        functools.partial(_l1_out_kernel, n_x_tiles=n_x_tiles),
        grid=(B, Tt // tm),
        in_specs=[row(D), row(D), pl.BlockSpec((1, tm, D), lambda b, t: (b, t, 4)),
                  row(D), full(mods1), full(ng), full(wo), full(g2), full(wqt)],
        out_specs=out_specs, out_shape=out_shape,
        compiler_params=_cparams(("parallel", "parallel"), VMEM_LIMIT),
        name="l1_out",
    )(of, ob, p, x2, mods1, ng, wo, g2, wqt)


def _final_kernel(x_ref, peer_ref, mods_ref, g_ref, o_ref):
    b = pl.program_id(0)
    row = mods_ref[pl.ds(b, 1), :]
    o_ref[0] = _rms(x_ref[0] + _mod(row, 5) * peer_ref[0], g_ref[...])


def _final(x3, peer, mods1, g, T):
    B = x3.shape[0]
    tm = ROW_TILE
    full = lambda a: pl.BlockSpec(a.shape, lambda b, t: (0,) * a.ndim)
    row = pl.BlockSpec((1, tm, D), lambda b, t: (b, t, 0))
    return pl.pallas_call(
        _final_kernel,
        grid=(B, T // tm),
        in_specs=[row, row, full(mods1), full(g)],
        out_specs=row,
        out_shape=jax.ShapeDtypeStruct((B, T, D), F32),
        compiler_params=_cparams(("parallel", "parallel")),
        name="final",
    )(x3, peer, mods1, g)


def _rope_tables(T, Tc):
    rows = T // GRID_W
    row = jnp.repeat(jnp.arange(rows, dtype=F32), GRID_W)
    col = jnp.tile(jnp.arange(GRID_W, dtype=F32), rows)
    axis_dim = MLA_ROPE // 2
    inv = ROPE_BASE ** (-jnp.arange(0, axis_dim, 2, dtype=F32) / axis_dim)
    ang = jnp.concatenate([row[:, None] * inv, col[:, None] * inv], axis=-1)
    cos = jnp.concatenate([jnp.cos(ang), jnp.ones((Tc, axis_dim), F32)], axis=0)
    sin = jnp.concatenate([jnp.sin(ang), jnp.zeros((Tc, axis_dim), F32)], axis=0)
    Tt = T + Tc
    ct = jnp.concatenate([jnp.ones((Tt, MLA_NOPE), F32), cos, cos, jnp.ones((Tt, 32), F32)], axis=1)
    st = jnp.concatenate([jnp.zeros((Tt, MLA_NOPE), F32), -sin, sin, jnp.zeros((Tt, 32), F32)], axis=1)
    return ct, st


def _mla_weights(w_in, q_g, w_uq, kv_g, w_ukv):
    z = lambda *s: jnp.zeros(s, F32)
    c1 = MLA_Q_RANK + MLA_KV_RANK
    kr = w_in[:, c1:c1 + MLA_ROPE]
    kr_e, kr_o = kr[:, 0::2], kr[:, 1::2]
    krblk = jnp.concatenate([z(D, MLA_NOPE), kr_e, kr_o, z(D, 32)], axis=1)
    krsblk = jnp.concatenate([z(D, MLA_NOPE), kr_o, kr_e, z(D, 32)], axis=1)
    w1 = jnp.concatenate([w_in[:, :c1], krblk, krsblk], axis=1)
    w2 = w_in[:, c1 + MLA_ROPE:]
    wu = w_uq.reshape(MLA_Q_RANK, MLA_HEADS, MLA_NOPE + MLA_ROPE)
    nope, r = wu[:, :, :MLA_NOPE], wu[:, :, MLA_NOPE:]
    re, ro = r[:, :, 0::2], r[:, :, 1::2]
    zq = lambda n: z(MLA_Q_RANK, MLA_HEADS, n)
    wq = jnp.concatenate([nope, re, ro, zq(32)], axis=-1).reshape(MLA_Q_RANK, -1)
    wqs = jnp.concatenate([zq(MLA_NOPE), ro, re, zq(32)], axis=-1).reshape(MLA_Q_RANK, -1)
    wkv = w_ukv.reshape(MLA_KV_RANK, MLA_HEADS, MLA_NOPE + MLA_V)
    zk = z(MLA_KV_RANK, MLA_HEADS, HEAD_PAD - MLA_NOPE)
    wk = jnp.concatenate([wkv[:, :, :MLA_NOPE], zk], axis=-1).reshape(MLA_KV_RANK, -1)
    wv = wkv[:, :, MLA_NOPE:].reshape(MLA_KV_RANK, -1)
    lane = jnp.arange(HEAD_PAD)
    place = jnp.where((lane >= MLA_NOPE) & (lane < MLA_NOPE + MLA_ROPE), 1.0, 0.0)
    e = jnp.tile(jnp.diag(place), (1, MLA_HEADS))
    bf = lambda a: a.astype(BF16)
    return dict(w1=bf(w1), w2=bf(w2), qg=q_g.reshape(1, -1), wq=bf(wq), wqs=bf(wqs),
                kvg=kv_g.reshape(1, -1), wk=bf(wk), wv=bf(wv), e=bf(e))


def _out_weights(w_out):
    return w_out[:MLA_HEADS * MLA_V].astype(BF16), w_out[MLA_HEADS * MLA_V:].astype(BF16)


def _peer(h2t, qt, sub_keys, u, v, B, dense):
    thr, ea, code, ebz = _route(qt, sub_keys.astype(BF16))
    return dense(h2t, u, v, thr, ea, code, ebz).reshape(B, -1, D)


def kernel(x, c, ctx, c_ctx, ada_w, ada_b, norm1_g, norm2_g, ab_w_in, mla_q_norm_g, mla_w_uq, mla_kv_norm_g, mla_w_ukv, conv_w, conv_b, conv_ln_g, conv_ln_b, ab_w_out, hg_w_in, hg_lower_bound, hg_norm_g, hg_w_out, peer_w_q, peer_sub_keys, peer_u, peer_v, final_norm_g):
    B, T, _ = x.shape
    Tc = ctx.shape[1]
    assert B <= 2 and T % ROW_TILE == 0 and Tc % ROW_TILE == 0 and ada_w.shape[0] == 2
    n_x_tiles = T // ROW_TILE
    row2 = lambda a: a.reshape(1, -1)

    xx = jnp.concatenate([x, ctx], axis=1)
    c8 = jnp.zeros((8, D), F32).at[:B].set(c).at[2].set(c_ctx)
    mods = _mods(c8, ada_w, ada_b)

    ct, st = _rope_tables(T, Tc)
    mw = _mla_weights(ab_w_in[0], mla_q_norm_g[0], mla_w_uq[0], mla_kv_norm_g[0], mla_w_ukv[0])
    q, k, v, y = _l0_in(xx, mods[0], row2(norm1_g[0]), mw, ct, st, n_x_tiles)
    o = _attention(q, k, v, T, Tc)
    cv = _conv(y, conv_w[0], row2(conv_b[0]), row2(conv_ln_g[0]), row2(conv_ln_b[0]), n_x_tiles)
    woa, woc = _out_weights(ab_w_out[0])
    x1, h2, qt = _l0_out(o, cv, xx, mods[0], woa, woc, row2(norm2_g[0]), peer_w_q[0].T.astype(BF16), n_x_tiles)
    peer0 = _peer(h2, qt, peer_sub_keys[0], peer_u[0], peer_v[0], B, _dense)

    x2, p = _l1_in(x1, peer0, mods[0], mods[1], row2(norm1_g[1]), hg_w_in[0].astype(BF16), n_x_tiles)
    of, ob = _gla(p, jnp.transpose(hg_lower_bound, (1, 0, 2)), T // HG_CHUNK, Tc // HG_CHUNK, layer=1)
    x3, h2, qt = _l1_out(of, ob, p, x2, mods[1], row2(hg_norm_g[0]), hg_w_out[0].astype(BF16),
                         row2(norm2_g[1]), peer_w_q[1].T.astype(BF16), n_x_tiles)
    peer1 = _peer(h2, qt, peer_sub_keys[1], peer_u[1], peer_v[1], B, _dense_flat)
    return _final(x3, peer1, mods[1], row2(final_norm_g), T)
```
